```python
import jax, jax.numpy as jnp
from jax import lax
import numpy as np


D_MODEL = 1024
BATCH = 4
SEQ = 8192
DEPTH = 2
DEC_BATCH = 32
DEC_SEQ = 2048
PAST_LEN = 128

D_PLE = 256
D_CONV = D_MODEL // 2
D_RWKV = D_MODEL // 2
HEAD_SIZE = 64
N_HEADS = D_RWKV // HEAD_SIZE
LORA_W = 64
LORA_A = 64
LORA_G = 128
D_FF = 11 * D_MODEL // 4
NORM_EPS = 1e-6
GN_EPS = HEAD_SIZE * 1e-5
IN_SIZES = (D_CONV, D_CONV, D_CONV, D_RWKV, D_RWKV, D_RWKV,
            LORA_W + LORA_A, LORA_W + LORA_A, LORA_G, D_MODEL, D_MODEL)
IN_COLS = sum(IN_SIZES)

kernel_name = 'hybrid_bidir_conv_rwkv7_encoder'


def _rmsnorm(x, g):
    xf = x.astype(jnp.float32)
    y = xf * lax.rsqrt(jnp.mean(xf * xf, axis=-1, keepdims=True) + NORM_EPS)
    return (y * g.astype(jnp.float32)).astype(x.dtype)


def _split(z, sizes):
    idx = [int(s) for s in np.cumsum(sizes)[:-1]]
    return jnp.split(z, idx, axis=-1)


def _shift_prev(z):
    return jnp.pad(z, ((0, 0), (1, 0), (0, 0)))[:, :-1]


def _shift_next(z):
    return jnp.pad(z, ((0, 0), (0, 1), (0, 0)))[:, 1:]


def _conv3(x, w, b):
    xp = jnp.pad(x, ((0, 0), (1, 1), (0, 0)))
    return xp[:, :-2] * w[0] + xp[:, 1:-1] * w[1] + xp[:, 2:] * w[2] + b


def _wkv_scan(r, w, kk, b, k, v, reverse):
    bsz = r.shape[0]
    xs = tuple(jnp.moveaxis(t.astype(jnp.float32), 1, 0) for t in (r, w, kk, b, k, v))

    def step(S, inp):
        r_t, w_t, kk_t, b_t, k_t, v_t = inp
        sa = jnp.einsum('bhvk,bhk->bhv', S, kk_t)
        S = (S * w_t[:, :, None, :] - sa[..., None] * b_t[:, :, None, :]
             + v_t[..., None] * k_t[:, :, None, :])
        return S, jnp.einsum('bhvk,bhk->bhv', S, r_t)

    S0 = jnp.zeros((bsz, N_HEADS, HEAD_SIZE, HEAD_SIZE), jnp.float32)
    _, ys = lax.scan(step, S0, xs, reverse=reverse)
    return jnp.moveaxis(ys, 0, 1)


def _rwkv_branch(r, k, v, zf, zb, gd, shift_mu, decay_w0, decay_w2, iclr_a0, iclr_a2,
                 gate_g2, k_k, k_a, r_k, gn_w, gn_b, w_branch_b):
    bsz, T, _ = r.shape
    f32 = jnp.float32

    def heads(t):
        return t.reshape(bsz, T, N_HEADS, HEAD_SIZE)

    r32, k32, v32 = r.astype(f32), k.astype(f32), v.astype(f32)
    kk = heads(k32 * k_k)
    kk = kk / jnp.maximum(jnp.linalg.norm(kk, axis=-1, keepdims=True), 1e-12)
    outs = []
    for d, (z, z_shift, rev) in enumerate(((zf, _shift_prev(zf), False),
                                           (zb, _shift_next(zb), True))):
        z = (z + shift_mu[d] * (z_shift - z)).astype(f32)
        zw, za = z[..., :LORA_W], z[..., LORA_W:]
        logit = decay_w0[d] + jnp.tanh(zw) @ decay_w2[d]
        w = jnp.exp(-jnp.exp(-jax.nn.softplus(-logit) - 0.5))
        a = jax.nn.sigmoid(iclr_a0[d] + za @ iclr_a2[d])
        kd = k32 * (1.0 + (a - 1.0) * k_a)
        outs.append(_wkv_scan(heads(r32), heads(w), kk, kk * heads(a), heads(kd),
                              heads(v32), rev))
    y = outs[0] + outs[1]
    mean = jnp.mean(y, axis=-1, keepdims=True)
    var = jnp.mean(jnp.square(y - mean), axis=-1, keepdims=True)
    y = ((y - mean) * lax.rsqrt(var + GN_EPS)).reshape(bsz, T, D_RWKV) * gn_w + gn_b
    bonus = jnp.sum(heads(r32) * heads(k32) * r_k, axis=-1, keepdims=True) * heads(v32)
    g = jax.nn.sigmoid(gd.astype(f32)) @ gate_g2
    out = (y + bonus.reshape(bsz, T, D_RWKV)) * g
    return out.astype(r.dtype) @ w_branch_b


def _mixer(u, w_in, conv_w, conv_b, w_branch_a, shift_mu, decay_w0, decay_w2, iclr_a0,
           iclr_a2, gate_g2, k_k, k_a, r_k, gn_w, gn_b, w_branch_b, w_out):
    proj = u @ w_in
    (hc, b_gate, c_gate, r, k, v, zf, zb, gd,
     gate_conv, gate_rwkv) = _split(proj, IN_SIZES)
    y_conv = (b_gate * _conv3(c_gate * hc, conv_w, conv_b)) @ w_branch_a
    y_rwkv = _rwkv_branch(r, k, v, zf, zb, gd, shift_mu, decay_w0, decay_w2, iclr_a0,
                          iclr_a2, gate_g2, k_k, k_a, r_k, gn_w, gn_b, w_branch_b)
    merged = jax.nn.sigmoid(gate_conv) * y_conv + jax.nn.sigmoid(gate_rwkv) * y_rwkv
    return merged @ w_out


def _conv_ffn(u, w_up, ffn_conv_w, ffn_conv_b, w_down):
    h = _conv3(u @ w_up, ffn_conv_w, ffn_conv_b)
    hg, hv = jnp.split(h, 2, axis=-1)
    return (jax.nn.gelu(hg, approximate=True) * hv) @ w_down


def _layer(x, p, norm_mix_pre, norm_mix_post, norm_ffn_pre, norm_ffn_post, norm_ple_post,
           w_in, conv_w, conv_b, w_branch_a, shift_mu, decay_w0, decay_w2, iclr_a0, iclr_a2,
           gate_g2, k_k, k_a, r_k, gn_w, gn_b, w_branch_b, w_out, w_up, ffn_conv_w,
           ffn_conv_b, w_down, w_ple, w_ple_gate):
    u = _rmsnorm(x, norm_mix_pre)
    m = _mixer(u, w_in, conv_w, conv_b, w_branch_a, shift_mu, decay_w0, decay_w2, iclr_a0,
               iclr_a2, gate_g2, k_k, k_a, r_k, gn_w, gn_b, w_branch_b, w_out)
    x = x + _rmsnorm(m, norm_mix_post)
    f = _conv_ffn(_rmsnorm(x, norm_ffn_pre), w_up, ffn_conv_w, ffn_conv_b, w_down)
    x = x + _rmsnorm(f, norm_ffn_post)
    gate = jax.nn.sigmoid(x @ w_ple_gate)
    x = x + _rmsnorm(gate * (p @ w_ple), norm_ple_post)
    return x


def setup_inputs(seed: int = 0) -> dict:
    key = jax.random.key(seed)
    ks = iter(jax.random.split(key, 48))
    L = DEPTH

    def nrm(shape, scale):
        return scale * jax.random.normal(next(ks), shape, jnp.float32)

    def gain(shape):
        return 1.0 + nrm(shape, 0.05)

    return {
        'x_prompt': nrm((BATCH, SEQ, D_MODEL), 1.0),
        'x_sample': nrm((DEC_BATCH, DEC_SEQ, D_MODEL), 1.0),
        'p_prompt': nrm((DEPTH, BATCH, SEQ, D_PLE), 1.0),
        'p_sample': nrm((DEPTH, DEC_BATCH, DEC_SEQ, D_PLE), 1.0),
        'norm_mix_pre': gain((L, D_MODEL)),
        'norm_mix_post': gain((L, D_MODEL)),
        'norm_ffn_pre': gain((L, D_MODEL)),
        'norm_ffn_post': gain((L, D_MODEL)),
        'norm_ple_post': gain((L, D_MODEL)),
        'w_in': nrm((L, D_MODEL, IN_COLS), D_MODEL ** -0.5),
        'conv_w': nrm((L, 3, D_CONV), 3 ** -0.5),
        'conv_b': nrm((L, D_CONV), 0.02),
        'w_branch_a': nrm((L, D_CONV, D_MODEL), D_CONV ** -0.5),
        'shift_mu': jax.random.uniform(next(ks), (L, 2, LORA_W + LORA_A), jnp.float32, 0.2, 0.8),
        'decay_w0': -3.0 + nrm((L, 2, D_RWKV), 1.5),
        'decay_w2': nrm((L, 2, LORA_W, D_RWKV), 0.1),
        'iclr_a0': nrm((L, 2, D_RWKV), 0.5),
        'iclr_a2': nrm((L, 2, LORA_A, D_RWKV), LORA_A ** -0.5),
        'gate_g2': nrm((L, LORA_G, D_RWKV), LORA_G ** -0.5),
        'k_k': 0.85 + nrm((L, D_RWKV), 0.05),
        'k_a': 1.0 + nrm((L, D_RWKV), 0.05),
        'r_k': nrm((L, N_HEADS, HEAD_SIZE), 0.1),
        'gn_w': gain((L, D_RWKV)),
        'gn_b': nrm((L, D_RWKV), 0.02),
        'w_branch_b': nrm((L, D_RWKV, D_MODEL), D_RWKV ** -0.5),
        'w_out': nrm((L, D_MODEL, D_MODEL), D_MODEL ** -0.5),
        'w_up': nrm((L, D_MODEL, 2 * D_FF), D_MODEL ** -0.5),
        'ffn_conv_w': nrm((L, 3, 2 * D_FF), 3 ** -0.5),
        'ffn_conv_b': nrm((L, 2 * D_FF), 0.02),
        'w_down': nrm((L, D_FF, D_MODEL), D_FF ** -0.5),
        'w_ple': nrm((L, D_PLE, D_MODEL), D_PLE ** -0.5),
        'w_ple_gate': nrm((L, D_MODEL, D_MODEL), D_MODEL ** -0.5),
    }


def reference(x_prompt, x_sample, p_prompt, p_sample, norm_mix_pre, norm_mix_post,
              norm_ffn_pre, norm_ffn_post, norm_ple_post, w_in, conv_w, conv_b, w_branch_a,
              shift_mu, decay_w0, decay_w2, iclr_a0, iclr_a2, gate_g2, k_k, k_a, r_k, gn_w,
              gn_b, w_branch_b, w_out, w_up, ffn_conv_w, ffn_conv_b, w_down, w_ple,
              w_ple_gate):
    y_prompt, y_sample = x_prompt, x_sample
    for i in range(DEPTH):
        lp = dict(norm_mix_pre=norm_mix_pre[i], norm_mix_post=norm_mix_post[i],
                  norm_ffn_pre=norm_ffn_pre[i], norm_ffn_post=norm_ffn_post[i],
                  norm_ple_post=norm_ple_post[i], w_in=w_in[i], conv_w=conv_w[i],
                  conv_b=conv_b[i], w_branch_a=w_branch_a[i], shift_mu=shift_mu[i],
                  decay_w0=decay_w0[i], decay_w2=decay_w2[i], iclr_a0=iclr_a0[i],
                  iclr_a2=iclr_a2[i], gate_g2=gate_g2[i], k_k=k_k[i], k_a=k_a[i],
                  r_k=r_k[i], gn_w=gn_w[i], gn_b=gn_b[i], w_branch_b=w_branch_b[i],
                  w_out=w_out[i], w_up=w_up[i], ffn_conv_w=ffn_conv_w[i],
                  ffn_conv_b=ffn_conv_b[i], w_down=w_down[i], w_ple=w_ple[i],
                  w_ple_gate=w_ple_gate[i])
        y_prompt = _layer(y_prompt, p_prompt[i], **lp)
        y_sample = _layer(y_sample, p_sample[i], **lp)
    return (y_prompt, y_sample)
```

```python
import functools
import math

import jax
import jax.numpy as jnp
from jax import lax
from jax.experimental import pallas as pl
from jax.experimental.pallas import tpu as pltpu

F32 = jnp.float32
BF16 = jnp.bfloat16

LANES = 128
SUBLANES = 8
BF16_ROWS = 16
VMEM_LIMIT_BYTES = 56 * 1024 * 1024

HEAD = 64
PAIR = 2 * HEAD
CHUNK = 64
NORM_EPS = 1e-6
GN_EPS = HEAD * 1e-5
DECAY_SCALE = math.exp(-0.5)
GELU_C = math.sqrt(2.0 / math.pi)


def _sigmoid(x):
    return 1.0 / (1.0 + jnp.exp(-x))


def _rms(x, g):
    return x * lax.rsqrt(jnp.mean(x * x, axis=-1, keepdims=True) + NORM_EPS) * g


def _dot(a, b):
    return jnp.dot(a, b, preferred_element_type=F32)


def _dot_nt(a, b):
    return lax.dot_general(a, b, (((1,), (1,)), ((), ())), preferred_element_type=F32)


def _dot_tn(a, b):
    return lax.dot_general(a, b, (((0,), (0,)), ((), ())), preferred_element_type=F32)


def _split2(x):
    hi = x.astype(BF16)
    lo = (x - hi.astype(F32)).astype(BF16)
    return hi, lo


def _head_sum(x, bd):
    hi, lo = _split2(x)
    return _dot(hi, bd) + _dot(lo, bd)


def _params(n_axes):
    return pltpu.CompilerParams(dimension_semantics=("arbitrary",) * n_axes,
                                vmem_limit_bytes=VMEM_LIMIT_BYTES)


def _full(shape):
    nd = len(shape)
    return pl.BlockSpec(shape, lambda *_: (0,) * nd)


def _in_proj_kernel(x_ref, g_ref, w_ref, cghc_ref, bg_ref, r_ref, k_ref, v_ref, z_ref, gd_ref,
                    sgc_ref, sgr_ref, *, d_conv, d_rwkv, d_z, d_g, d_model):
    u = _rms(x_ref[...], g_ref[...]).astype(BF16)
    o = 0
    hbc = _dot(u, w_ref[:, o:o + 3 * d_conv])
    cghc_ref[...] = (hbc[:, 2 * d_conv:] * hbc[:, :d_conv]).astype(BF16)
    bg_ref[...] = hbc[:, d_conv:2 * d_conv].astype(BF16)
    o += 3 * d_conv
    for ref in (r_ref, k_ref, v_ref):
        ref[...] = _dot(u, w_ref[:, o:o + d_rwkv])
        o += d_rwkv
    z_ref[...] = _dot(u, w_ref[:, o:o + d_z])
    o += d_z
    gd_ref[...] = _dot(u, w_ref[:, o:o + d_g])
    o += d_g
    for ref in (sgc_ref, sgr_ref):
        ref[...] = _sigmoid(_dot(u, w_ref[:, o:o + d_model])).astype(BF16)
        o += d_model


def _in_proj(x, g, w_in, *, tm, d_conv, d_rwkv, d_z, d_g):
    n, d_model = x.shape
    cols = w_in.shape[1]
    row = lambda w: pl.BlockSpec((tm, w), lambda i: (i, 0))
    widths = (d_conv, d_conv, d_rwkv, d_rwkv, d_rwkv, d_z, d_g, d_model, d_model)
    dtypes = (BF16, BF16, F32, F32, F32, F32, F32, BF16, BF16)
    return pl.pallas_call(
        functools.partial(_in_proj_kernel, d_conv=d_conv, d_rwkv=d_rwkv, d_z=d_z, d_g=d_g,
                          d_model=d_model),
        grid=(n // tm,),
        in_specs=[row(d_model), _full((1, d_model)), _full((d_model, cols))],
        out_specs=[row(w) for w in widths],
        out_shape=[jax.ShapeDtypeStruct((n, w), dt) for w, dt in zip(widths, dtypes)],
        compiler_params=_params(1),
        name="in_proj",
    )(x, g, w_in)


def _wkv_kernel(r_ref, k_ref, v_ref, z_ref, mu_ref, w0_ref, a0_ref, w2_ref, kk_ref, ka_ref, bd_ref,
                y_ref, lw_s, kkn_s, b_s, kd_s, state_s, zc_s, *, reverse, tb, n_pairs):
    f32 = F32
    d = n_pairs * PAIR
    nch = tb // CHUNK

    @pl.when(pl.program_id(1) == 0)
    def _():
        state_s[...] = jnp.zeros_like(state_s)
        zc_s[...] = jnp.zeros_like(zc_s)

    z = z_ref[...]
    rows = lax.broadcasted_iota(jnp.int32, z.shape, 0)
    carry = jnp.broadcast_to(zc_s[0:1, :], z.shape)
    if reverse:
        zs = jnp.where(rows == tb - 1, carry, pltpu.roll(z, tb - 1, 0))
        zc_s[...] = jnp.broadcast_to(z[0:1, :], zc_s.shape)
    else:
        zs = jnp.where(rows == 0, carry, pltpu.roll(z, 1, 0))
        zc_s[...] = jnp.broadcast_to(z[tb - 1:tb, :], zc_s.shape)
    zm = z + mu_ref[...] * (zs - z)
    lanes = lax.broadcasted_iota(jnp.int32, z.shape, 1)
    feat = jnp.where(lanes < HEAD, jnp.tanh(zm), zm).astype(BF16)
    lo = _dot(feat, w2_ref[...])
    lw_s[...] = -DECAY_SCALE * _sigmoid(w0_ref[...] + lo[:, :d])
    a = _sigmoid(a0_ref[...] + lo[:, d:])
    k = k_ref[...]
    kkr = k * kk_ref[...]
    nrm = jnp.sqrt(_head_sum(kkr * kkr, bd_ref[...]))
    kkn = kkr / jnp.maximum(nrm, 1e-12)
    kkn_s[...] = kkn
    b_s[...] = kkn * a
    kd_s[...] = k * (1.0 + (a - 1.0) * ka_ref[...])

    ti = lax.broadcasted_iota(jnp.int32, (CHUNK, PAIR), 0)
    ci = lax.broadcasted_iota(jnp.int32, (CHUNK, PAIR), 1)
    si = ci & (HEAD - 1)
    if reverse:
        strict, incl = si > ti, si >= ti
    else:
        strict, incl = si < ti, si <= ti
    eye = (si == ti).astype(f32)
    li = lax.broadcasted_iota(jnp.int32, (CHUNK, CHUNK), 0)
    lj = lax.broadcasted_iota(jnp.int32, (CHUNK, CHUNK), 1)
    cum = ((lj >= li) if reverse else (lj <= li)).astype(BF16)
    head_lo = lax.broadcasted_iota(jnp.int32, (CHUNK, PAIR), 1) < HEAD
    bi = lax.broadcasted_iota(jnp.int32, (PAIR, PAIR), 0)
    bj = lax.broadcasted_iota(jnp.int32, (PAIR, PAIR), 1)
    same_head = (bi < HEAD) == (bj < HEAD)
    levels = []
    sz = 1
    while sz < CHUNK:
        same_blk = (ti & -(2 * sz)) == (si & -(2 * sz))
        t_hi, s_hi = (ti & sz) != 0, (si & sz) != 0
        levels.append(same_blk & ((~t_hi & s_hi) if reverse else (t_hi & ~s_hi)))
        sz *= 2
    edge = 0 if reverse else CHUNK - 1

    def stack(x):
        return jnp.concatenate([jnp.where(head_lo, x, 0.0), jnp.where(head_lo, 0.0, x)],
                               axis=0).astype(BF16)

    def chunk_step(c, _):
        cc = (nch - 1 - c) if reverse else c
        r0 = pl.multiple_of(cc * CHUNK, CHUNK)
        for p in range(n_pairs):
            cols = slice(p * PAIR, (p + 1) * PAIR)
            lw = lw_s[pl.ds(r0, CHUNK), cols]
            r = r_ref[pl.ds(r0, CHUNK), cols]
            v = v_ref[pl.ds(r0, CHUNK), cols]
            kk = kkn_s[pl.ds(r0, CHUNK), cols]
            b = b_s[pl.ds(r0, CHUNK), cols]
            kd = kd_s[pl.ds(r0, CHUNK), cols]
            h1 = lw.astype(BF16)
            r1 = lw - h1.astype(f32)
            h2 = r1.astype(BF16)
            h3 = (r1 - h2.astype(f32)).astype(BF16)
            cs = _dot(cum, jnp.concatenate([h1, h2, h3], axis=1))
            cw = cs[:, :PAIR] + cs[:, PAIR:2 * PAIR] + cs[:, 2 * PAIR:]
            e_pos = jnp.exp(cw)
            e_neg = jnp.exp(-cw)
            rw = r * e_pos
            kkw = kk * jnp.exp(cw - lw)
            binv = b * e_neg
            kinv = kd * e_neg
            aa = _dot_nt(jnp.concatenate([kkw, rw], axis=0).astype(BF16),
                         jnp.concatenate([stack(binv), stack(kinv)], axis=0))
            a_ab = jnp.where(strict, aa[:CHUNK, :PAIR], 0.0)
            a_ak = jnp.where(strict, aa[:CHUNK, PAIR:], 0.0)
            a_rb = jnp.where(incl, aa[CHUNK:, :PAIR], 0.0)
            a_rk = jnp.where(incl, aa[CHUNK:, PAIR:], 0.0)
            tinv = eye - jnp.where(levels[0], a_ab, 0.0)
            for lvl in levels[1:]:
                m1 = _dot(tinv.astype(BF16), stack(jnp.where(lvl, a_ab, 0.0)))
                tinv = tinv - _dot(m1.astype(BF16), stack(tinv))
            av = _dot(jnp.concatenate([a_ak, a_rk], axis=0).astype(BF16), stack(v))
            akv, y0 = av[:CHUNK], av[CHUNK:]
            tt = _dot(tinv.astype(BF16), jnp.concatenate([stack(kkw), stack(akv)], axis=1))
            p1, p2 = tt[:, :PAIR], tt[:, PAIR:]
            s_old = state_s[p]
            ps = _dot_nt(jnp.concatenate([p1, rw], axis=0).astype(BF16), s_old.astype(BF16))
            u = -(ps[:CHUNK] + p2)
            y = ps[CHUNK:] + _dot(a_rb.astype(BF16), stack(u)) + y0
            y_ref[pl.ds(r0, CHUNK), cols] = y
            w_end = e_pos[edge:edge + 1, :]
            upd = _dot_tn(jnp.concatenate([u, v], axis=0).astype(BF16),
                          jnp.concatenate([binv * w_end, kinv * w_end], axis=0).astype(BF16))
            state_s[p] = s_old * w_end + jnp.where(same_head, upd, 0.0)
        return 0

    lax.fori_loop(0, nch, chunk_step, 0)


def _wkv(r, k, v, z, mu, w0, a0, w2, k_k, k_a, bd, *, reverse, bsz, seq, tb):
    n, d = r.shape
    nt = seq // tb
    n_pairs = d // PAIR
    if reverse:
        tmap = lambda b, i: (b * nt + nt - 1 - i, 0)
    else:
        tmap = lambda b, i: (b * nt + i, 0)
    zcol = 1 if reverse else 0
    zmap = lambda b, i: (tmap(b, i)[0], zcol)
    blk = pl.BlockSpec((tb, d), tmap)
    return pl.pallas_call(
        functools.partial(_wkv_kernel, reverse=reverse, tb=tb, n_pairs=n_pairs),
        grid=(bsz, nt),
        in_specs=[blk, blk, blk, pl.BlockSpec((tb, PAIR), zmap),
                  _full((1, PAIR)), _full((1, d)), _full((1, d)), _full((PAIR, 2 * d)),
                  _full((1, d)), _full((1, d)), _full((d, d))],
        out_specs=blk,
        out_shape=jax.ShapeDtypeStruct((n, d), F32),
        scratch_shapes=[pltpu.VMEM((tb, d), F32)] * 4
        + [pltpu.VMEM((n_pairs, PAIR, PAIR), F32), pltpu.VMEM((SUBLANES, PAIR), F32)],
        compiler_params=_params(2),
        name="wkv_bwd" if reverse else "wkv_fwd",
    )(r, k, v, z, mu, w0, a0, w2, k_k, k_a, bd)


def _conv3(x, prev_row, next_row, w_ref, b_ref, cols, first, last):
    m = x.shape[0]
    rows = lax.broadcasted_iota(jnp.int32, x.shape, 0)
    prev_row = jnp.where(first, 0.0, prev_row)
    next_row = jnp.where(last, 0.0, next_row)
    xp = jnp.where(rows == 0, jnp.broadcast_to(prev_row, x.shape), pltpu.roll(x, 1, 0))
    xn = jnp.where(rows == m - 1, jnp.broadcast_to(next_row, x.shape), pltpu.roll(x, m - 1, 0))
    return (xp * w_ref[0:1, cols] + x * w_ref[1:2, cols] + xn * w_ref[2:3, cols] + b_ref[:, cols])


def _halo_specs(tm, width, rows_per_blk, n_rows):
    nb = tm // rows_per_blk
    last_blk = n_rows // rows_per_blk - 1
    prev = pl.BlockSpec((rows_per_blk, width), lambda i: (jnp.maximum(i * nb - 1, 0), 0))
    nxt = pl.BlockSpec((rows_per_blk, width), lambda i: (jnp.minimum((i + 1) * nb, last_blk), 0))
    return prev, nxt


def _mix_out_kernel(x_ref, yf_ref, yb_ref, r_ref, k_ref, v_ref, gd_ref, c_ref, cp_ref, cn_ref, bg_ref,
                    sgc_ref, sgr_ref, cw_ref, cb_ref, wa_ref, g2_ref, rk_ref, gnw_ref, gnb_ref,
                    bd_ref, wb_ref, wo_ref, g_ref, o_ref, *, tm, seq):
    i = pl.program_id(0)
    first = (i * tm) % seq == 0
    last = ((i + 1) * tm) % seq == 0
    c = c_ref[...].astype(F32)
    conv = _conv3(c, cp_ref[BF16_ROWS - 1:BF16_ROWS, :].astype(F32), cn_ref[0:1, :].astype(F32),
                  cw_ref, cb_ref, slice(None), first, last)
    y_conv = _dot((bg_ref[...].astype(F32) * conv).astype(BF16), wa_ref[...])
    bd = bd_ref[...]
    y = yf_ref[...] + yb_ref[...]
    mean = _head_sum(y, bd) * (1.0 / HEAD)
    yc = y - mean
    var = _head_sum(yc * yc, bd) * (1.0 / HEAD)
    yn = yc * lax.rsqrt(var + GN_EPS) * gnw_ref[...] + gnb_ref[...]
    bonus = _head_sum(r_ref[...] * k_ref[...] * rk_ref[...], bd) * v_ref[...]
    gate = _dot(_sigmoid(gd_ref[...]).astype(BF16), g2_ref[...])
    y_rwkv = _dot(((yn + bonus) * gate).astype(BF16), wb_ref[...])
    merged = sgc_ref[...].astype(F32) * y_conv + sgr_ref[...].astype(F32) * y_rwkv
    m = _dot(merged.astype(BF16), wo_ref[...])
    o_ref[...] = x_ref[...] + _rms(m, g_ref[...])


def _mix_out(x, yf, yb, r, k, v, gd, cghc, bgate, sgc, sgr, conv_w, conv_b, w_a, g2, r_k, gn_w, gn_b,
             bd, w_b, w_out, g_post, *, tm, seq):
    n, d_model = x.shape
    d = r.shape[1]
    d_conv = cghc.shape[1]
    row = lambda w: pl.BlockSpec((tm, w), lambda i: (i, 0))
    cp, cn = _halo_specs(tm, d_conv, BF16_ROWS, n)
    return pl.pallas_call(
        functools.partial(_mix_out_kernel, tm=tm, seq=seq),
        grid=(n // tm,),
        in_specs=[row(d_model), row(d), row(d), row(d), row(d), row(d), row(gd.shape[1]),
                  row(d_conv), cp, cn, row(d_conv), row(d_model), row(d_model),
                  _full(conv_w.shape), _full(conv_b.shape), _full(w_a.shape), _full(g2.shape),
                  _full(r_k.shape), _full(gn_w.shape), _full(gn_b.shape), _full(bd.shape),
                  _full(w_b.shape), _full(w_out.shape), _full(g_post.shape)],
        out_specs=row(d_model),
        out_shape=jax.ShapeDtypeStruct((n, d_model), F32),
        compiler_params=_params(1),
        name="mix_out",
    )(x, yf, yb, r, k, v, gd, cghc, cghc, cghc, bgate, sgc, sgr, conv_w, conv_b, w_a, g2, r_k, gn_w,
      gn_b, bd, w_b, w_out, g_post)


def _ffn_up_kernel(x_ref, g_ref, w_ref, o_ref, *, col_chunk):
    u = _rms(x_ref[...], g_ref[...]).astype(BF16)
    for c0 in range(0, w_ref.shape[1], col_chunk):
        o_ref[:, c0:c0 + col_chunk] = _dot(u, w_ref[:, c0:c0 + col_chunk]).astype(BF16)


def _ffn_up(x, g, w_up, *, tm, col_chunk):
    n, d_model = x.shape
    cols = w_up.shape[1]
    return pl.pallas_call(
        functools.partial(_ffn_up_kernel, col_chunk=col_chunk),
        grid=(n // tm,),
        in_specs=[pl.BlockSpec((tm, d_model), lambda i: (i, 0)), _full((1, d_model)),
                  _full((d_model, cols))],
        out_specs=pl.BlockSpec((tm, cols), lambda i: (i, 0)),
        out_shape=jax.ShapeDtypeStruct((n, cols), BF16),
        compiler_params=_params(1),
        name="ffn_up",
    )(x, g, w_up)


def _ffn_down_kernel(x_ref, p_ref, h_ref, hp_ref, hn_ref, cw_ref, cb_ref, wd_ref, gf_ref, wp_ref, wg_ref,
                     gp_ref, o_ref, act_s, *, tm, seq, d_ff, col_chunk):
    i = pl.program_id(0)
    first = (i * tm) % seq == 0
    last = ((i + 1) * tm) % seq == 0
    for c0 in range(0, d_ff, col_chunk):
        halves = []
        for off in (c0, d_ff + c0):
            cols = slice(off, off + col_chunk)
            halves.append(_conv3(h_ref[:, cols].astype(F32),
                                 hp_ref[BF16_ROWS - 1:BF16_ROWS, cols].astype(F32),
                                 hn_ref[0:1, cols].astype(F32), cw_ref, cb_ref, cols, first, last))
        hg, hv = halves
        gelu = 0.5 * hg * (1.0 + jnp.tanh(GELU_C * (hg + 0.044715 * (hg * hg * hg))))
        act_s[:, c0:c0 + col_chunk] = (gelu * hv).astype(BF16)
    f = _dot(act_s[...], wd_ref[...])
    x = x_ref[...] + _rms(f, gf_ref[...])
    gate = _sigmoid(_dot(x.astype(BF16), wg_ref[...]))
    pe = _dot(p_ref[...].astype(BF16), wp_ref[...])
    o_ref[...] = x + _rms(gate * pe, gp_ref[...])


def _ffn_down(x, p, h, conv_w, conv_b, w_down, g_ffn, w_ple, w_gate, g_ple, *, tm, seq, col_chunk):
    n, d_model = x.shape
    d_ff = w_down.shape[0]
    row = lambda w: pl.BlockSpec((tm, w), lambda i: (i, 0))
    hp, hn = _halo_specs(tm, 2 * d_ff, BF16_ROWS, n)
    return pl.pallas_call(
        functools.partial(_ffn_down_kernel, tm=tm, seq=seq, d_ff=d_ff, col_chunk=col_chunk),
        grid=(n // tm,),
        in_specs=[row(d_model), row(p.shape[1]), row(2 * d_ff), hp, hn, _full(conv_w.shape),
                  _full(conv_b.shape), _full(w_down.shape), _full(g_ffn.shape), _full(w_ple.shape),
                  _full(w_gate.shape), _full(g_ple.shape)],
        out_specs=row(d_model),
        out_shape=jax.ShapeDtypeStruct((n, d_model), F32),
        scratch_shapes=[pltpu.VMEM((tm, d_ff), BF16)],
        compiler_params=_params(1),
        name="ffn_down",
    )(x, p, h, h, h, conv_w, conv_b, w_down, g_ffn, w_ple, w_gate, g_ple)


def _tiles(seq):
    tm = min(256, seq)
    tb = min(256, seq)
    return tm, tb


def _layer_weights(i, norm_mix_pre, norm_mix_post, norm_ffn_pre, norm_ffn_post, norm_ple_post, w_in,
                   conv_w, conv_b, w_branch_a, shift_mu, decay_w0, decay_w2, iclr_a0, iclr_a2, gate_g2,
                   k_k, k_a, r_k, gn_w, gn_b, w_branch_b, w_out, w_up, ffn_conv_w, ffn_conv_b, w_down,
                   w_ple, w_ple_gate):
    d_rwkv = k_k.shape[1]
    lora = decay_w2.shape[2]
    row = lambda a: a[i].reshape(1, -1)
    zeros = jnp.zeros((lora, d_rwkv), F32)
    lowrank = [jnp.concatenate([jnp.concatenate([decay_w2[i, d], zeros], axis=1),
                                jnp.concatenate([zeros, iclr_a2[i, d]], axis=1)], axis=0).astype(BF16)
               for d in range(2)]
    head_id = jnp.arange(d_rwkv) // HEAD
    return dict(
        g_mix_pre=row(norm_mix_pre), g_mix_post=row(norm_mix_post), g_ffn_pre=row(norm_ffn_pre),
        g_ffn_post=row(norm_ffn_post), g_ple_post=row(norm_ple_post),
        w_in=w_in[i].astype(BF16), conv_w=conv_w[i], conv_b=row(conv_b),
        w_a=w_branch_a[i].astype(BF16),
        mu=[shift_mu[i, d].reshape(1, -1) for d in range(2)],
        w0=[decay_w0[i, d].reshape(1, -1) for d in range(2)],
        a0=[iclr_a0[i, d].reshape(1, -1) for d in range(2)],
        lowrank=lowrank, g2=gate_g2[i].astype(BF16), k_k=row(k_k), k_a=row(k_a),
        r_k=r_k[i].reshape(1, -1), gn_w=row(gn_w), gn_b=row(gn_b),
        bd=(head_id[:, None] == head_id[None, :]).astype(BF16),
        w_b=w_branch_b[i].astype(BF16), w_out=w_out[i].astype(BF16), w_up=w_up[i].astype(BF16),
        ffn_conv_w=ffn_conv_w[i], ffn_conv_b=row(ffn_conv_b), w_down=w_down[i].astype(BF16),
        w_ple=w_ple[i].astype(BF16), w_gate=w_ple_gate[i].astype(BF16))


def _layer(x, p, lw, *, bsz, seq):
    tm, tb = _tiles(seq)
    d_conv = lw["conv_w"].shape[1]
    d_rwkv = lw["k_k"].shape[1]
    d_z = lw["mu"][0].shape[1]
    d_g = lw["g2"].shape[0]
    d_ff = lw["w_down"].shape[0]
    cghc, bgate, r, k, v, z, gd, sgc, sgr = _in_proj(
        x, lw["g_mix_pre"], lw["w_in"], tm=tm, d_conv=d_conv, d_rwkv=d_rwkv, d_z=2 * d_z, d_g=d_g)
    ys = [_wkv(r, k, v, z, lw["mu"][d], lw["w0"][d], lw["a0"][d], lw["lowrank"][d], lw["k_k"],
               lw["k_a"], lw["bd"], reverse=bool(d), bsz=bsz, seq=seq, tb=tb) for d in range(2)]
    x = _mix_out(x, ys[0], ys[1], r, k, v, gd, cghc, bgate, sgc, sgr, lw["conv_w"], lw["conv_b"],
                 lw["w_a"], lw["g2"], lw["r_k"], lw["gn_w"], lw["gn_b"], lw["bd"], lw["w_b"],
                 lw["w_out"], lw["g_mix_post"], tm=tm, seq=seq)
    h = _ffn_up(x, lw["g_ffn_pre"], lw["w_up"], tm=tm, col_chunk=math.gcd(2 * d_ff, 512))
    return _ffn_down(x, p, h, lw["ffn_conv_w"], lw["ffn_conv_b"], lw["w_down"], lw["g_ffn_post"],
                     lw["w_ple"], lw["w_gate"], lw["g_ple_post"], tm=tm, seq=seq,
                     col_chunk=math.gcd(d_ff, 256))


def kernel(x_prompt, x_sample, p_prompt, p_sample, norm_mix_pre, norm_mix_post, norm_ffn_pre, norm_ffn_post, norm_ple_post, w_in, conv_w, conv_b, w_branch_a, shift_mu, decay_w0, decay_w2, iclr_a0, iclr_a2, gate_g2, k_k, k_a, r_k, gn_w, gn_b, w_branch_b, w_out, w_up, ffn_conv_w, ffn_conv_b, w_down, w_ple, w_ple_gate):
    weights = (norm_mix_pre, norm_mix_post, norm_ffn_pre, norm_ffn_post, norm_ple_post, w_in, conv_w,
               conv_b, w_branch_a, shift_mu, decay_w0, decay_w2, iclr_a0, iclr_a2, gate_g2, k_k, k_a,
               r_k, gn_w, gn_b, w_branch_b, w_out, w_up, ffn_conv_w, ffn_conv_b, w_down, w_ple,
               w_ple_gate)
    layers = [_layer_weights(i, *weights) for i in range(w_in.shape[0])]
    outs = []
    for x, p in ((x_prompt, p_prompt), (x_sample, p_sample)):
        bsz, seq, d_model = x.shape
        y = x.reshape(bsz * seq, d_model)
        for i, lw in enumerate(layers):
            y = _layer(y, p[i].reshape(bsz * seq, -1), lw, bsz=bsz, seq=seq)
        outs.append(y.reshape(bsz, seq, d_model))
    return tuple(outs)
```

```python
import functools
import math

import jax
import jax.numpy as jnp
from jax import lax
from jax.experimental import pallas as pl
from jax.experimental.pallas import tpu as pltpu

F32 = jnp.float32
BF16 = jnp.bfloat16

LANES = 128
SUBLANES = 8
BF16_ROWS = 16
VMEM_LIMIT_BYTES = 56 * 1024 * 1024

HEAD = 64
PAIR = 2 * HEAD
CHUNK = 64
NORM_EPS = 1e-6
GN_EPS = HEAD * 1e-5
DECAY_SCALE = math.exp(-0.5)
GELU_C = math.sqrt(2.0 / math.pi)


def _sigmoid(x):
    return 1.0 / (1.0 + jnp.exp(-x))


def _rms(x, g):
    return x * lax.rsqrt(jnp.mean(x * x, axis=-1, keepdims=True) + NORM_EPS) * g


def _dot(a, b):
    return jnp.dot(a, b, preferred_element_type=F32)


def _dot_nt(a, b):
    return lax.dot_general(a, b, (((1,), (1,)), ((), ())), preferred_element_type=F32)


def _dot_tn(a, b):
    return lax.dot_general(a, b, (((0,), (0,)), ((), ())), preferred_element_type=F32)


def _split2(x):
    hi = x.astype(BF16)
    lo = (x - hi.astype(F32)).astype(BF16)
    return hi, lo


def _head_sum(x, bd):
    hi, lo = _split2(x)
    return _dot(hi, bd) + _dot(lo, bd)


def _params(n_axes):
    return pltpu.CompilerParams(dimension_semantics=("arbitrary",) * n_axes,
                                vmem_limit_bytes=VMEM_LIMIT_BYTES)


def _full(shape):
    nd = len(shape)
    return pl.BlockSpec(shape, lambda *_: (0,) * nd)


def _in_proj_kernel(x_ref, g_ref, w_ref, cghc_ref, bg_ref, r_ref, k_ref, v_ref, z_ref, gd_ref,
                    sgc_ref, sgr_ref, *, d_conv, d_rwkv, d_z, d_g, d_model):
    u = _rms(x_ref[...], g_ref[...]).astype(BF16)
    o = 0
    hbc = _dot(u, w_ref[:, o:o + 3 * d_conv])
    cghc_ref[...] = (hbc[:, 2 * d_conv:] * hbc[:, :d_conv]).astype(BF16)
    bg_ref[...] = hbc[:, d_conv:2 * d_conv].astype(BF16)
    o += 3 * d_conv
    for ref in (r_ref, k_ref, v_ref):
        ref[...] = _dot(u, w_ref[:, o:o + d_rwkv])
        o += d_rwkv
    z_ref[...] = _dot(u, w_ref[:, o:o + d_z])
    o += d_z
    gd_ref[...] = _dot(u, w_ref[:, o:o + d_g])
    o += d_g
    for ref in (sgc_ref, sgr_ref):
        ref[...] = _sigmoid(_dot(u, w_ref[:, o:o + d_model])).astype(BF16)
        o += d_model


def _in_proj(x, g, w_in, *, tm, d_conv, d_rwkv, d_z, d_g):
    n, d_model = x.shape
    cols = w_in.shape[1]
    row = lambda w: pl.BlockSpec((tm, w), lambda i: (i, 0))
    widths = (d_conv, d_conv, d_rwkv, d_rwkv, d_rwkv, d_z, d_g, d_model, d_model)
    dtypes = (BF16, BF16, F32, F32, F32, F32, F32, BF16, BF16)
    return pl.pallas_call(
        functools.partial(_in_proj_kernel, d_conv=d_conv, d_rwkv=d_rwkv, d_z=d_z, d_g=d_g,
                          d_model=d_model),
        grid=(n // tm,),
        in_specs=[row(d_model), _full((1, d_model)), _full((d_model, cols))],
        out_specs=[row(w) for w in widths],
        out_shape=[jax.ShapeDtypeStruct((n, w), dt) for w, dt in zip(widths, dtypes)],
        compiler_params=_params(1),
        name="in_proj",
    )(x, g, w_in)


def _wkv_masks(reverse):
    ti = lax.broadcasted_iota(jnp.int32, (CHUNK, PAIR), 0)
    si = lax.broadcasted_iota(jnp.int32, (CHUNK, PAIR), 1) & (HEAD - 1)
    strict = (si > ti) if reverse else (si < ti)
    incl = (si >= ti) if reverse else (si <= ti)
    li = lax.broadcasted_iota(jnp.int32, (CHUNK, CHUNK), 0)
    lj = lax.broadcasted_iota(jnp.int32, (CHUNK, CHUNK), 1)
    levels = []
    sz = 1
    while sz < CHUNK:
        same_blk = (ti & -(2 * sz)) == (si & -(2 * sz))
        t_hi, s_hi = (ti & sz) != 0, (si & sz) != 0
        lvl = (same_blk & ((~t_hi & s_hi) if reverse else (t_hi & ~s_hi))).astype(BF16)
        levels.append(lvl if sz == 1 else jnp.concatenate([lvl, lvl], axis=0))
        sz *= 2
    strict, incl = strict.astype(BF16), incl.astype(BF16)
    return dict(
        cum=((lj >= li) if reverse else (lj <= li)).astype(BF16),
        aa=jnp.concatenate([jnp.concatenate([strict, strict], axis=1),
                            jnp.concatenate([incl, incl], axis=1)], axis=0),
        eye=(si == ti).astype(F32), levels=levels, edge=0 if reverse else CHUNK - 1)


def _wkv_prep(z_ref, k_ref, mu, w0, a0, w2, k_k, k_a, bd, zc_s, lw_s, kkn_s, b_s, kd_s, *, reverse, tb, d):
    z = z_ref[...]
    rows = lax.broadcasted_iota(jnp.int32, z.shape, 0)
    carry = jnp.broadcast_to(zc_s[0:1, :], z.shape)
    if reverse:
        zs = jnp.where(rows == tb - 1, carry, pltpu.roll(z, tb - 1, 0))
        zc_s[...] = jnp.broadcast_to(z[0:1, :], zc_s.shape)
    else:
        zs = jnp.where(rows == 0, carry, pltpu.roll(z, 1, 0))
        zc_s[...] = jnp.broadcast_to(z[tb - 1:tb, :], zc_s.shape)
    zm = z + mu * (zs - z)
    lanes = lax.broadcasted_iota(jnp.int32, z.shape, 1)
    feat = jnp.where(lanes < HEAD, jnp.tanh(zm), zm).astype(BF16)
    lo = _dot(feat, w2)
    lw_s[...] = -DECAY_SCALE * _sigmoid(w0 + lo[:, :d])
    a = _sigmoid(a0 + lo[:, d:])
    k = k_ref[...]
    kkr = k * k_k
    nrm = jnp.sqrt(_head_sum(kkr * kkr, bd))
    kkn = kkr / jnp.maximum(nrm, 1e-12)
    kkn_s[...] = kkn
    b_s[...] = kkn * a
    kd_s[...] = k * (1.0 + (a - 1.0) * k_a)


def _wkv_kernel(rf_ref, kf_ref, vf_ref, zf_ref, rb_ref, kb_ref, vb_ref, zb_ref, mu_ref, w0_ref, a0_ref,
                w2_ref, kk_ref, ka_ref, bd_ref, yf_ref, yb_ref, lw_s, kkn_s, b_s, kd_s, state_s, zc_s,
                *, tb, n_pairs):
    d = n_pairs * PAIR
    nch = tb // CHUNK
    r_refs, k_refs, v_refs, z_refs, y_refs = ((rf_ref, rb_ref), (kf_ref, kb_ref), (vf_ref, vb_ref),
                                              (zf_ref, zb_ref), (yf_ref, yb_ref))

    @pl.when(pl.program_id(1) == 0)
    def _():
        state_s[...] = jnp.zeros_like(state_s)
        zc_s[...] = jnp.zeros_like(zc_s)

    for dr in range(2):
        _wkv_prep(z_refs[dr], k_refs[dr], mu_ref[dr], w0_ref[dr], a0_ref[dr], w2_ref[dr], kk_ref[...],
                  ka_ref[...], bd_ref[...], zc_s.at[dr], lw_s.at[dr], kkn_s.at[dr], b_s.at[dr],
                  kd_s.at[dr], reverse=bool(dr), tb=tb, d=d)

    masks = [_wkv_masks(False), _wkv_masks(True)]
    head_lo = (lax.broadcasted_iota(jnp.int32, (CHUNK, PAIR), 1) < HEAD).astype(BF16)
    head_hi = 1.0 - head_lo
    bi = lax.broadcasted_iota(jnp.int32, (PAIR, PAIR), 0)
    bj = lax.broadcasted_iota(jnp.int32, (PAIR, PAIR), 1)
    same_head = (bi < HEAD) == (bj < HEAD)
    units = [(dr, p) for dr in range(2) for p in range(n_pairs)]

    def stack(x):
        xb = x.astype(BF16)
        return jnp.concatenate([xb * head_lo, xb * head_hi], axis=0)

    def chunk_step(c, _):
        row0 = (pl.multiple_of(c * CHUNK, CHUNK), pl.multiple_of((nch - 1 - c) * CHUNK, CHUNK))

        def load(refs_or_scratch, dr, p, scratch=False):
            src = refs_or_scratch.at[dr] if scratch else refs_or_scratch[dr]
            return src[pl.ds(row0[dr], CHUNK), p * PAIR:(p + 1) * PAIR]

        lw = [load(lw_s, dr, p, True) for dr, p in units]
        cw = []
        for (dr, p), x in zip(units, lw):
            h1 = x.astype(BF16)
            r1 = x - h1.astype(F32)
            h2 = r1.astype(BF16)
            h3 = (r1 - h2.astype(F32)).astype(BF16)
            cs = _dot(masks[dr]["cum"], jnp.concatenate([h1, h2, h3], axis=1))
            cw.append(cs[:, :PAIR] + cs[:, PAIR:2 * PAIR] + cs[:, 2 * PAIR:])
        e_pos = [jnp.exp(x) for x in cw]
        e_neg = [jnp.exp(-x) for x in cw]
        rw = [load(r_refs, dr, p) * e for (dr, p), e in zip(units, e_pos)]
        kkw = [load(kkn_s, dr, p, True) * jnp.exp(x - y) for (dr, p), x, y in zip(units, cw, lw)]
        binv = [load(b_s, dr, p, True) * e for (dr, p), e in zip(units, e_neg)]
        kinv = [load(kd_s, dr, p, True) * e for (dr, p), e in zip(units, e_neg)]
        v = [load(v_refs, dr, p) for dr, p in units]
        aa = [(_dot_nt(jnp.concatenate([x, y], axis=0).astype(BF16),
                       jnp.concatenate([stack(bi_), stack(ki_)], axis=0)).astype(BF16)
               * masks[dr]["aa"])
              for (dr, p), x, y, bi_, ki_ in zip(units, kkw, rw, binv, kinv)]
        a_ab = [x[:CHUNK, :PAIR] for x in aa]
        a_rb = [x[CHUNK:, :PAIR] for x in aa]
        a_kv = [x[:, PAIR:] for x in aa]
        tinv = [masks[dr]["eye"] - (x * masks[dr]["levels"][0]).astype(F32)
                for (dr, p), x in zip(units, a_ab)]
        a_st = [jnp.concatenate([x * head_lo, x * head_hi], axis=0) for x in a_ab]
        for lvl in range(1, len(masks[0]["levels"])):
            m1 = [_dot(t.astype(BF16), x * masks[dr]["levels"][lvl])
                  for (dr, p), t, x in zip(units, tinv, a_st)]
            tinv = [t - _dot(m.astype(BF16), stack(t)) for t, m in zip(tinv, m1)]
        av = [_dot(x, stack(y)) for x, y in zip(a_kv, v)]
        tt = [_dot(t.astype(BF16), jnp.concatenate([stack(x), stack(y[:CHUNK])], axis=1))
              for t, x, y in zip(tinv, kkw, av)]
        s_old = [state_s[dr * n_pairs + p] for dr, p in units]
        ps = [_dot_nt(jnp.concatenate([t[:, :PAIR], x], axis=0).astype(BF16), s.astype(BF16))
              for t, x, s in zip(tt, rw, s_old)]
        u = [-(x[:CHUNK] + t[:, PAIR:]) for x, t in zip(ps, tt)]
        y = [x[CHUNK:] + _dot(a, stack(w)) + z[CHUNK:] for x, a, w, z in zip(ps, a_rb, u, av)]
        for (dr, p), x in zip(units, y):
            y_refs[dr][pl.ds(row0[dr], CHUNK), p * PAIR:(p + 1) * PAIR] = x
        w_end = [e[masks[dr]["edge"]:masks[dr]["edge"] + 1, :] for (dr, p), e in zip(units, e_pos)]
        upd = [_dot_tn(jnp.concatenate([x, y_], axis=0).astype(BF16),
                       jnp.concatenate([bi_ * w, ki_ * w], axis=0).astype(BF16))
               for x, y_, bi_, ki_, w in zip(u, v, binv, kinv, w_end)]
        for (dr, p), s, w, x in zip(units, s_old, w_end, upd):
            state_s[dr * n_pairs + p] = s * w + jnp.where(same_head, x, 0.0)
        return 0

    lax.fori_loop(0, nch, chunk_step, 0)


def _wkv(r, k, v, z, mu, w0, a0, w2, k_k, k_a, bd, *, bsz, seq, tb):
    n, d = r.shape
    nt = seq // tb
    n_pairs = d // PAIR
    fmap = lambda b, i: (b * nt + i, 0)
    bmap = lambda b, i: (b * nt + nt - 1 - i, 0)
    fblk, bblk = pl.BlockSpec((tb, d), fmap), pl.BlockSpec((tb, d), bmap)
    zf = pl.BlockSpec((tb, PAIR), fmap)
    zb = pl.BlockSpec((tb, PAIR), lambda b, i: (bmap(b, i)[0], 1))
    return pl.pallas_call(
        functools.partial(_wkv_kernel, tb=tb, n_pairs=n_pairs),
        grid=(bsz, nt),
        in_specs=[fblk, fblk, fblk, zf, bblk, bblk, bblk, zb,
                  _full(mu.shape), _full(w0.shape), _full(a0.shape), _full(w2.shape),
                  _full(k_k.shape), _full(k_a.shape), _full(bd.shape)],
        out_specs=[fblk, bblk],
        out_shape=[jax.ShapeDtypeStruct((n, d), F32)] * 2,
        scratch_shapes=[pltpu.VMEM((2, tb, d), F32)] * 4
        + [pltpu.VMEM((2 * n_pairs, PAIR, PAIR), F32), pltpu.VMEM((2, SUBLANES, PAIR), F32)],
        compiler_params=_params(2),
        name="wkv",
    )(r, k, v, z, r, k, v, z, mu, w0, a0, w2, k_k, k_a, bd)


def _conv3(x, prev_row, next_row, w_ref, b_ref, cols, first, last):
    m = x.shape[0]
    rows = lax.broadcasted_iota(jnp.int32, x.shape, 0)
    prev_row = jnp.where(first, 0.0, prev_row)
    next_row = jnp.where(last, 0.0, next_row)
    xp = jnp.where(rows == 0, jnp.broadcast_to(prev_row, x.shape), pltpu.roll(x, 1, 0))
    xn = jnp.where(rows == m - 1, jnp.broadcast_to(next_row, x.shape), pltpu.roll(x, m - 1, 0))
    return (xp * w_ref[0:1, cols] + x * w_ref[1:2, cols] + xn * w_ref[2:3, cols] + b_ref[:, cols])


def _halo_specs(tm, width, rows_per_blk, n_rows):
    nb = tm // rows_per_blk
    last_blk = n_rows // rows_per_blk - 1
    prev = pl.BlockSpec((rows_per_blk, width), lambda i: (jnp.maximum(i * nb - 1, 0), 0))
    nxt = pl.BlockSpec((rows_per_blk, width), lambda i: (jnp.minimum((i + 1) * nb, last_blk), 0))
    return prev, nxt


def _mix_out_kernel(x_ref, yf_ref, yb_ref, r_ref, k_ref, v_ref, gd_ref, c_ref, cp_ref, cn_ref, bg_ref,
                    sgc_ref, sgr_ref, cw_ref, cb_ref, wa_ref, g2_ref, rk_ref, gnw_ref, gnb_ref,
                    bd_ref, wb_ref, wo_ref, g_ref, o_ref, *, tm, seq):
    i = pl.program_id(0)
    first = (i * tm) % seq == 0
    last = ((i + 1) * tm) % seq == 0
    c = c_ref[...].astype(F32)
    conv = _conv3(c, cp_ref[BF16_ROWS - 1:BF16_ROWS, :].astype(F32), cn_ref[0:1, :].astype(F32),
                  cw_ref, cb_ref, slice(None), first, last)
    y_conv = _dot((bg_ref[...].astype(F32) * conv).astype(BF16), wa_ref[...])
    bd = bd_ref[...]
    y = yf_ref[...] + yb_ref[...]
    mean = _head_sum(y, bd) * (1.0 / HEAD)
    yc = y - mean
    var = _head_sum(yc * yc, bd) * (1.0 / HEAD)
    yn = yc * lax.rsqrt(var + GN_EPS) * gnw_ref[...] + gnb_ref[...]
    bonus = _head_sum(r_ref[...] * k_ref[...] * rk_ref[...], bd) * v_ref[...]
    gate = _dot(_sigmoid(gd_ref[...]).astype(BF16), g2_ref[...])
    y_rwkv = _dot(((yn + bonus) * gate).astype(BF16), wb_ref[...])
    merged = sgc_ref[...].astype(F32) * y_conv + sgr_ref[...].astype(F32) * y_rwkv
    m = _dot(merged.astype(BF16), wo_ref[...])
    o_ref[...] = x_ref[...] + _rms(m, g_ref[...])


def _mix_out(x, yf, yb, r, k, v, gd, cghc, bgate, sgc, sgr, conv_w, conv_b, w_a, g2, r_k, gn_w, gn_b,
             bd, w_b, w_out, g_post, *, tm, seq):
    n, d_model = x.shape
    d = r.shape[1]
    d_conv = cghc.shape[1]
    row = lambda w: pl.BlockSpec((tm, w), lambda i: (i, 0))
    cp, cn = _halo_specs(tm, d_conv, BF16_ROWS, n)
    return pl.pallas_call(
        functools.partial(_mix_out_kernel, tm=tm, seq=seq),
        grid=(n // tm,),
        in_specs=[row(d_model), row(d), row(d), row(d), row(d), row(d), row(gd.shape[1]),
                  row(d_conv), cp, cn, row(d_conv), row(d_model), row(d_model),
                  _full(conv_w.shape), _full(conv_b.shape), _full(w_a.shape), _full(g2.shape),
                  _full(r_k.shape), _full(gn_w.shape), _full(gn_b.shape), _full(bd.shape),
                  _full(w_b.shape), _full(w_out.shape), _full(g_post.shape)],
        out_specs=row(d_model),
        out_shape=jax.ShapeDtypeStruct((n, d_model), F32),
        compiler_params=_params(1),
        name="mix_out",
    )(x, yf, yb, r, k, v, gd, cghc, cghc, cghc, bgate, sgc, sgr, conv_w, conv_b, w_a, g2, r_k, gn_w,
      gn_b, bd, w_b, w_out, g_post)


def _ffn_up_kernel(x_ref, g_ref, w_ref, o_ref, *, col_chunk):
    u = _rms(x_ref[...], g_ref[...]).astype(BF16)
    for c0 in range(0, w_ref.shape[1], col_chunk):
        o_ref[:, c0:c0 + col_chunk] = _dot(u, w_ref[:, c0:c0 + col_chunk]).astype(BF16)


def _ffn_up(x, g, w_up, *, tm, col_chunk):
    n, d_model = x.shape
    cols = w_up.shape[1]
    return pl.pallas_call(
        functools.partial(_ffn_up_kernel, col_chunk=col_chunk),
        grid=(n // tm,),
        in_specs=[pl.BlockSpec((tm, d_model), lambda i: (i, 0)), _full((1, d_model)),
                  _full((d_model, cols))],
        out_specs=pl.BlockSpec((tm, cols), lambda i: (i, 0)),
        out_shape=jax.ShapeDtypeStruct((n, cols), BF16),
        compiler_params=_params(1),
        name="ffn_up",
    )(x, g, w_up)


def _ffn_down_kernel(x_ref, p_ref, h_ref, hp_ref, hn_ref, cw_ref, cb_ref, wd_ref, gf_ref, wp_ref, wg_ref,
                     gp_ref, o_ref, act_s, *, tm, seq, d_ff, col_chunk):
    i = pl.program_id(0)
    first = (i * tm) % seq == 0
    last = ((i + 1) * tm) % seq == 0
    for c0 in range(0, d_ff, col_chunk):
        halves = []
        for off in (c0, d_ff + c0):
            cols = slice(off, off + col_chunk)
            halves.append(_conv3(h_ref[:, cols].astype(F32),
                                 hp_ref[BF16_ROWS - 1:BF16_ROWS, cols].astype(F32),
                                 hn_ref[0:1, cols].astype(F32), cw_ref, cb_ref, cols, first, last))
        hg, hv = halves
        gelu = 0.5 * hg * (1.0 + jnp.tanh(GELU_C * (hg + 0.044715 * (hg * hg * hg))))
        act_s[:, c0:c0 + col_chunk] = (gelu * hv).astype(BF16)
    f = _dot(act_s[...], wd_ref[...])
    x = x_ref[...] + _rms(f, gf_ref[...])
    gate = _sigmoid(_dot(x.astype(BF16), wg_ref[...]))
    pe = _dot(p_ref[...].astype(BF16), wp_ref[...])
    o_ref[...] = x + _rms(gate * pe, gp_ref[...])


def _ffn_down(x, p, h, conv_w, conv_b, w_down, g_ffn, w_ple, w_gate, g_ple, *, tm, seq, col_chunk):
    n, d_model = x.shape
    d_ff = w_down.shape[0]
    row = lambda w: pl.BlockSpec((tm, w), lambda i: (i, 0))
    hp, hn = _halo_specs(tm, 2 * d_ff, BF16_ROWS, n)
    return pl.pallas_call(
        functools.partial(_ffn_down_kernel, tm=tm, seq=seq, d_ff=d_ff, col_chunk=col_chunk),
        grid=(n // tm,),
        in_specs=[row(d_model), row(p.shape[1]), row(2 * d_ff), hp, hn, _full(conv_w.shape),
                  _full(conv_b.shape), _full(w_down.shape), _full(g_ffn.shape), _full(w_ple.shape),
                  _full(w_gate.shape), _full(g_ple.shape)],
        out_specs=row(d_model),
        out_shape=jax.ShapeDtypeStruct((n, d_model), F32),
        scratch_shapes=[pltpu.VMEM((tm, d_ff), BF16)],
        compiler_params=_params(1),
        name="ffn_down",
    )(x, p, h, h, h, conv_w, conv_b, w_down, g_ffn, w_ple, w_gate, g_ple)


def _tiles(seq):
    tm = min(256, seq)
    tb = min(256, seq)
    return tm, tb


def _layer_weights(i, norm_mix_pre, norm_mix_post, norm_ffn_pre, norm_ffn_post, norm_ple_post, w_in,
                   conv_w, conv_b, w_branch_a, shift_mu, decay_w0, decay_w2, iclr_a0, iclr_a2, gate_g2,
                   k_k, k_a, r_k, gn_w, gn_b, w_branch_b, w_out, w_up, ffn_conv_w, ffn_conv_b, w_down,
                   w_ple, w_ple_gate):
    d_rwkv = k_k.shape[1]
    lora = decay_w2.shape[2]
    row = lambda a: a[i].reshape(1, -1)
    zeros = jnp.zeros((lora, d_rwkv), F32)
    lowrank = [jnp.concatenate([jnp.concatenate([decay_w2[i, d], zeros], axis=1),
                                jnp.concatenate([zeros, iclr_a2[i, d]], axis=1)], axis=0).astype(BF16)
               for d in range(2)]
    head_id = jnp.arange(d_rwkv) // HEAD
    return dict(
        g_mix_pre=row(norm_mix_pre), g_mix_post=row(norm_mix_post), g_ffn_pre=row(norm_ffn_pre),
        g_ffn_post=row(norm_ffn_post), g_ple_post=row(norm_ple_post),
        w_in=w_in[i].astype(BF16), conv_w=conv_w[i], conv_b=row(conv_b),
        w_a=w_branch_a[i].astype(BF16),
        mu=shift_mu[i][:, None, :], w0=decay_w0[i][:, None, :], a0=iclr_a0[i][:, None, :],
        lowrank=jnp.stack(lowrank), g2=gate_g2[i].astype(BF16), k_k=row(k_k), k_a=row(k_a),
        r_k=r_k[i].reshape(1, -1), gn_w=row(gn_w), gn_b=row(gn_b),
        bd=(head_id[:, None] == head_id[None, :]).astype(BF16),
        w_b=w_branch_b[i].astype(BF16), w_out=w_out[i].astype(BF16), w_up=w_up[i].astype(BF16),
        ffn_conv_w=ffn_conv_w[i], ffn_conv_b=row(ffn_conv_b), w_down=w_down[i].astype(BF16),
        w_ple=w_ple[i].astype(BF16), w_gate=w_ple_gate[i].astype(BF16))


def _layer(x, p, lw, *, bsz, seq):
    tm, tb = _tiles(seq)
    d_conv = lw["conv_w"].shape[1]
    d_rwkv = lw["k_k"].shape[1]
    d_z = lw["mu"].shape[2]
    d_g = lw["g2"].shape[0]
    d_ff = lw["w_down"].shape[0]
    cghc, bgate, r, k, v, z, gd, sgc, sgr = _in_proj(
        x, lw["g_mix_pre"], lw["w_in"], tm=tm, d_conv=d_conv, d_rwkv=d_rwkv, d_z=2 * d_z, d_g=d_g)
    ys = _wkv(r, k, v, z, lw["mu"], lw["w0"], lw["a0"], lw["lowrank"], lw["k_k"], lw["k_a"], lw["bd"],
              bsz=bsz, seq=seq, tb=tb)
    x = _mix_out(x, ys[0], ys[1], r, k, v, gd, cghc, bgate, sgc, sgr, lw["conv_w"], lw["conv_b"],
                 lw["w_a"], lw["g2"], lw["r_k"], lw["gn_w"], lw["gn_b"], lw["bd"], lw["w_b"],
                 lw["w_out"], lw["g_mix_post"], tm=tm, seq=seq)
    h = _ffn_up(x, lw["g_ffn_pre"], lw["w_up"], tm=tm, col_chunk=math.gcd(2 * d_ff, 512))
    return _ffn_down(x, p, h, lw["ffn_conv_w"], lw["ffn_conv_b"], lw["w_down"], lw["g_ffn_post"],
                     lw["w_ple"], lw["w_gate"], lw["g_ple_post"], tm=tm, seq=seq,
                     col_chunk=math.gcd(d_ff, 256))


def kernel(x_prompt, x_sample, p_prompt, p_sample, norm_mix_pre, norm_mix_post, norm_ffn_pre, norm_ffn_post, norm_ple_post, w_in, conv_w, conv_b, w_branch_a, shift_mu, decay_w0, decay_w2, iclr_a0, iclr_a2, gate_g2, k_k, k_a, r_k, gn_w, gn_b, w_branch_b, w_out, w_up, ffn_conv_w, ffn_conv_b, w_down, w_ple, w_ple_gate):
    weights = (norm_mix_pre, norm_mix_post, norm_ffn_pre, norm_ffn_post, norm_ple_post, w_in, conv_w,
               conv_b, w_branch_a, shift_mu, decay_w0, decay_w2, iclr_a0, iclr_a2, gate_g2, k_k, k_a,
               r_k, gn_w, gn_b, w_branch_b, w_out, w_up, ffn_conv_w, ffn_conv_b, w_down, w_ple,
               w_ple_gate)
    layers = [_layer_weights(i, *weights) for i in range(w_in.shape[0])]
    outs = []
    for x, p in ((x_prompt, p_prompt), (x_sample, p_sample)):
        bsz, seq, d_model = x.shape
        y = x.reshape(bsz * seq, d_model)
        for i, lw in enumerate(layers):
            y = _layer(y, p[i].reshape(bsz * seq, -1), lw, bsz=bsz, seq=seq)
        outs.append(y.reshape(bsz, seq, d_model))
    return tuple(outs)
```

```python
import functools
import math

import jax
import jax.numpy as jnp
from jax import lax
from jax.experimental import pallas as pl
from jax.experimental.pallas import tpu as pltpu

F32 = jnp.float32
BF16 = jnp.bfloat16

LANES = 128
SUBLANES = 8
BF16_ROWS = 16
VMEM_LIMIT_BYTES = 56 * 1024 * 1024

HEAD = 64
PAIR = 2 * HEAD
CHUNK = 64
CHUNKS_PER_STEP = 2
CONV_ROWS = 128
NORM_EPS = 1e-6
GN_EPS = HEAD * 1e-5
DECAY_SCALE = math.exp(-0.5)
GELU_C = math.sqrt(2.0 / math.pi)


def _sigmoid(x):
    return 1.0 / (1.0 + jnp.exp(-x))


def _rms(x, g):
    return x * lax.rsqrt(jnp.mean(x * x, axis=-1, keepdims=True) + NORM_EPS) * g


def _dot(a, b):
    return jnp.dot(a, b, preferred_element_type=F32)


def _dot_nt(a, b):
    return lax.dot_general(a, b, (((1,), (1,)), ((), ())), preferred_element_type=F32)


def _dot_tn(a, b):
    return lax.dot_general(a, b, (((0,), (0,)), ((), ())), preferred_element_type=F32)


def _split2(x):
    hi = x.astype(BF16)
    lo = (x - hi.astype(F32)).astype(BF16)
    return hi, lo


def _head_sum(x, bd):
    hi, lo = _split2(x)
    return _dot(hi, bd) + _dot(lo, bd)


def _params(n_axes):
    return pltpu.CompilerParams(dimension_semantics=("arbitrary",) * n_axes,
                                vmem_limit_bytes=VMEM_LIMIT_BYTES)


def _full(shape):
    nd = len(shape)
    return pl.BlockSpec(shape, lambda *_: (0,) * nd)


def _in_proj_kernel(x_ref, g_ref, w_ref, cghc_ref, bg_ref, r_ref, k_ref, v_ref, z_ref, gd_ref,
                    sgc_ref, sgr_ref, *, d_conv, d_rwkv, d_z, d_g, d_model):
    u = _rms(x_ref[...], g_ref[...]).astype(BF16)
    o = 0
    hbc = _dot(u, w_ref[:, o:o + 3 * d_conv])
    cghc_ref[...] = (hbc[:, 2 * d_conv:] * hbc[:, :d_conv]).astype(BF16)
    bg_ref[...] = hbc[:, d_conv:2 * d_conv].astype(BF16)
    o += 3 * d_conv
    for ref in (r_ref, k_ref, v_ref):
        ref[...] = _dot(u, w_ref[:, o:o + d_rwkv])
        o += d_rwkv
    z_ref[...] = _dot(u, w_ref[:, o:o + d_z])
    o += d_z
    gd_ref[...] = _dot(u, w_ref[:, o:o + d_g])
    o += d_g
    for ref in (sgc_ref, sgr_ref):
        ref[...] = _sigmoid(_dot(u, w_ref[:, o:o + d_model])).astype(BF16)
        o += d_model


def _in_proj(x, g, w_in, *, tm, d_conv, d_rwkv, d_z, d_g):
    n, d_model = x.shape
    cols = w_in.shape[1]
    row = lambda w: pl.BlockSpec((tm, w), lambda i: (i, 0))
    widths = (d_conv, d_conv, d_rwkv, d_rwkv, d_rwkv, d_z, d_g, d_model, d_model)
    dtypes = (BF16, BF16, F32, F32, F32, F32, F32, BF16, BF16)
    return pl.pallas_call(
        functools.partial(_in_proj_kernel, d_conv=d_conv, d_rwkv=d_rwkv, d_z=d_z, d_g=d_g,
                          d_model=d_model),
        grid=(n // tm,),
        in_specs=[row(d_model), _full((1, d_model)), _full((d_model, cols))],
        out_specs=[row(w) for w in widths],
        out_shape=[jax.ShapeDtypeStruct((n, w), dt) for w, dt in zip(widths, dtypes)],
        compiler_params=_params(1),
        name="in_proj",
    )(x, g, w_in)


def _wkv_masks(reverse):
    ti = lax.broadcasted_iota(jnp.int32, (CHUNK, PAIR), 0)
    si = lax.broadcasted_iota(jnp.int32, (CHUNK, PAIR), 1) & (HEAD - 1)
    strict = (si > ti) if reverse else (si < ti)
    incl = (si >= ti) if reverse else (si <= ti)
    li = lax.broadcasted_iota(jnp.int32, (CHUNK, CHUNK), 0)
    lj = lax.broadcasted_iota(jnp.int32, (CHUNK, CHUNK), 1)
    levels = []
    sz = 1
    while sz < CHUNK:
        same_blk = (ti & -(2 * sz)) == (si & -(2 * sz))
        t_hi, s_hi = (ti & sz) != 0, (si & sz) != 0
        lvl = (same_blk & ((~t_hi & s_hi) if reverse else (t_hi & ~s_hi))).astype(BF16)
        levels.append(lvl if sz == 1 else jnp.concatenate([lvl, lvl], axis=0))
        sz *= 2
    strict, incl = strict.astype(BF16), incl.astype(BF16)
    return dict(
        cum=((lj >= li) if reverse else (lj <= li)).astype(BF16),
        aa=jnp.concatenate([jnp.concatenate([strict, strict], axis=1),
                            jnp.concatenate([incl, incl], axis=1)], axis=0),
        eye=(si == ti).astype(F32), levels=levels, edge=0 if reverse else CHUNK - 1)


def _wkv_prep(z_ref, k_ref, mu, w0, a0, w2, k_k, k_a, bd, zc_s, lw_s, kkn_s, b_s, kd_s, *, reverse, tb, d):
    z = z_ref[...]
    rows = lax.broadcasted_iota(jnp.int32, z.shape, 0)
    carry = jnp.broadcast_to(zc_s[0:1, :], z.shape)
    if reverse:
        zs = jnp.where(rows == tb - 1, carry, pltpu.roll(z, tb - 1, 0))
        zc_s[...] = jnp.broadcast_to(z[0:1, :], zc_s.shape)
    else:
        zs = jnp.where(rows == 0, carry, pltpu.roll(z, 1, 0))
        zc_s[...] = jnp.broadcast_to(z[tb - 1:tb, :], zc_s.shape)
    zm = z + mu * (zs - z)
    lanes = lax.broadcasted_iota(jnp.int32, z.shape, 1)
    feat = jnp.where(lanes < HEAD, jnp.tanh(zm), zm).astype(BF16)
    lo = _dot(feat, w2)
    lw_s[...] = -DECAY_SCALE * _sigmoid(w0 + lo[:, :d])
    a = _sigmoid(a0 + lo[:, d:])
    k = k_ref[...]
    kkr = k * k_k
    kkn = kkr * lax.rsqrt(jnp.maximum(_head_sum(kkr * kkr, bd), 1e-24))
    kkn_s[...] = kkn
    b_s[...] = kkn * a
    kd_s[...] = k * (1.0 + (a - 1.0) * k_a)


def _wkv_kernel(rf_ref, kf_ref, vf_ref, zf_ref, rb_ref, kb_ref, vb_ref, zb_ref, mu_ref, w0_ref, a0_ref,
                w2_ref, kk_ref, ka_ref, bd_ref, yf_ref, yb_ref, lw_s, kkn_s, b_s, kd_s, state_s, zc_s,
                *, tb, n_pairs):
    d = n_pairs * PAIR
    nch = tb // CHUNK
    r_refs, k_refs, v_refs, z_refs, y_refs = ((rf_ref, rb_ref), (kf_ref, kb_ref), (vf_ref, vb_ref),
                                              (zf_ref, zb_ref), (yf_ref, yb_ref))

    @pl.when(pl.program_id(1) == 0)
    def _():
        state_s[...] = jnp.zeros_like(state_s)
        zc_s[...] = jnp.zeros_like(zc_s)

    for dr in range(2):
        _wkv_prep(z_refs[dr], k_refs[dr], mu_ref[dr], w0_ref[dr], a0_ref[dr], w2_ref[dr], kk_ref[...],
                  ka_ref[...], bd_ref[...], zc_s.at[dr], lw_s.at[dr], kkn_s.at[dr], b_s.at[dr],
                  kd_s.at[dr], reverse=bool(dr), tb=tb, d=d)

    masks = [_wkv_masks(False), _wkv_masks(True)]
    head_lo = (lax.broadcasted_iota(jnp.int32, (CHUNK, PAIR), 1) < HEAD).astype(BF16)
    head_hi = 1.0 - head_lo
    bi = lax.broadcasted_iota(jnp.int32, (PAIR, PAIR), 0)
    bj = lax.broadcasted_iota(jnp.int32, (PAIR, PAIR), 1)
    same_head = (bi < HEAD) == (bj < HEAD)

    def stack(x):
        xb = x.astype(BF16)
        return jnp.concatenate([xb * head_lo, xb * head_hi], axis=0)

    group = math.gcd(nch, CHUNKS_PER_STEP)
    units = [(dr, p, j) for j in range(group) for dr in range(2) for p in range(n_pairs)]

    def chunk_step(c, _):
        row0 = [(pl.multiple_of((c * group + j) * CHUNK, CHUNK),
                 pl.multiple_of((nch - 1 - c * group - j) * CHUNK, CHUNK)) for j in range(group)]

        def load(refs_or_scratch, dr, p, j, scratch=False):
            src = refs_or_scratch.at[dr] if scratch else refs_or_scratch[dr]
            return src[pl.ds(row0[j][dr], CHUNK), p * PAIR:(p + 1) * PAIR]

        lw = [load(lw_s, dr, p, j, True) for dr, p, j in units]
        cw = []
        for (dr, p, j), x in zip(units, lw):
            h1, h2 = _split2(x)
            cs = _dot(masks[dr]["cum"], jnp.concatenate([h1, h2], axis=1))
            cw.append(cs[:, :PAIR] + cs[:, PAIR:])
        e_pos = [jnp.exp(x) for x in cw]
        e_neg = [jnp.exp(-x) for x in cw]
        rw = [load(r_refs, dr, p, j) * e for (dr, p, j), e in zip(units, e_pos)]
        kkw = [load(kkn_s, dr, p, j, True) * jnp.exp(x - y) for (dr, p, j), x, y in zip(units, cw, lw)]
        binv = [load(b_s, dr, p, j, True) * e for (dr, p, j), e in zip(units, e_neg)]
        kinv = [load(kd_s, dr, p, j, True) * e for (dr, p, j), e in zip(units, e_neg)]
        v = [load(v_refs, dr, p, j) for dr, p, j in units]
        aa = [(_dot_nt(jnp.concatenate([x, y], axis=0).astype(BF16),
                       jnp.concatenate([stack(bi_), stack(ki_)], axis=0)).astype(BF16)
               * masks[dr]["aa"])
              for (dr, p, j), x, y, bi_, ki_ in zip(units, kkw, rw, binv, kinv)]
        a_ab = [x[:CHUNK, :PAIR] for x in aa]
        a_rb = [x[CHUNK:, :PAIR] for x in aa]
        a_kv = [x[:, PAIR:] for x in aa]
        tinv = [masks[dr]["eye"] - (x * masks[dr]["levels"][0]).astype(F32)
                for (dr, p, j), x in zip(units, a_ab)]
        a_st = [jnp.concatenate([x * head_lo, x * head_hi], axis=0) for x in a_ab]
        for lvl in range(1, len(masks[0]["levels"])):
            m1 = [_dot(t.astype(BF16), x * masks[dr]["levels"][lvl])
                  for (dr, p, j), t, x in zip(units, tinv, a_st)]
            tinv = [t - _dot(m.astype(BF16), stack(t)) for t, m in zip(tinv, m1)]
        av = [_dot(x, stack(y)) for x, y in zip(a_kv, v)]
        tt = [_dot(t.astype(BF16), jnp.concatenate([stack(x), stack(y[:CHUNK])], axis=1))
              for t, x, y in zip(tinv, kkw, av)]
        w_end = [e[masks[dr]["edge"]:masks[dr]["edge"] + 1, :] for (dr, p, j), e in zip(units, e_pos)]
        bk_end = [jnp.concatenate([bi_ * w, ki_ * w], axis=0).astype(BF16)
                  for bi_, ki_, w in zip(binv, kinv, w_end)]

        state = [state_s[dr * n_pairs + p] for dr, p, j in units if j == 0]
        for jj in range(group):
            idx = [i for i, (dr, p, j) in enumerate(units) if j == jj]
            ps = [_dot_nt(jnp.concatenate([tt[i][:, :PAIR], rw[i]], axis=0).astype(BF16), s.astype(BF16))
                  for i, s in zip(idx, state)]
            u = [-(x[:CHUNK] + tt[i][:, PAIR:]) for i, x in zip(idx, ps)]
            y = [x[CHUNK:] + _dot(a_rb[i], stack(w)) + av[i][CHUNK:] for i, x, w in zip(idx, ps, u)]
            for i, x in zip(idx, y):
                dr, p, j = units[i]
                y_refs[dr][pl.ds(row0[j][dr], CHUNK), p * PAIR:(p + 1) * PAIR] = x
            upd = [_dot_tn(jnp.concatenate([x, v[i]], axis=0).astype(BF16), bk_end[i])
                   for i, x in zip(idx, u)]
            state = [s * w_end[i] + jnp.where(same_head, x, 0.0) for i, s, x in zip(idx, state, upd)]
        for (dr, p, j), s in zip([un for un in units if un[2] == 0], state):
            state_s[dr * n_pairs + p] = s
        return 0

    lax.fori_loop(0, nch // group, chunk_step, 0)


def _wkv(r, k, v, z, mu, w0, a0, w2, k_k, k_a, bd, *, bsz, seq, tb):
    n, d = r.shape
    nt = seq // tb
    n_pairs = d // PAIR
    fmap = lambda b, i: (b * nt + i, 0)
    bmap = lambda b, i: (b * nt + nt - 1 - i, 0)
    fblk, bblk = pl.BlockSpec((tb, d), fmap), pl.BlockSpec((tb, d), bmap)
    zf = pl.BlockSpec((tb, PAIR), fmap)
    zb = pl.BlockSpec((tb, PAIR), lambda b, i: (bmap(b, i)[0], 1))
    return pl.pallas_call(
        functools.partial(_wkv_kernel, tb=tb, n_pairs=n_pairs),
        grid=(bsz, nt),
        in_specs=[fblk, fblk, fblk, zf, bblk, bblk, bblk, zb,
                  _full(mu.shape), _full(w0.shape), _full(a0.shape), _full(w2.shape),
                  _full(k_k.shape), _full(k_a.shape), _full(bd.shape)],
        out_specs=[fblk, bblk],
        out_shape=[jax.ShapeDtypeStruct((n, d), F32)] * 2,
        scratch_shapes=[pltpu.VMEM((2, tb, d), F32)] * 4
        + [pltpu.VMEM((2 * n_pairs, PAIR, PAIR), F32), pltpu.VMEM((2, SUBLANES, PAIR), F32)],
        compiler_params=_params(2),
        name="wkv",
    )(r, k, v, z, r, k, v, z, mu, w0, a0, w2, k_k, k_a, bd)


def _shift_matrix(rows):
    shape = (2 * rows, rows + 2 * BF16_ROWS)
    r = lax.broadcasted_iota(jnp.int32, shape, 0)
    c = lax.broadcasted_iota(jnp.int32, shape, 1)
    target = jnp.where(r < rows, r + (BF16_ROWS - 1), r - rows + (BF16_ROWS + 1))
    return (c == target).astype(BF16)


def _conv3(x_ref, prev_ref, next_ref, shift, w_ref, b_ref, cols, first, last, r0):
    m = x_ref.shape[0]
    rows = shift.shape[0] // 2
    lo, hi = r0 - BF16_ROWS, r0 + rows + BF16_ROWS
    pieces = [x_ref[max(lo, 0):min(hi, m), cols]]
    if lo < 0:
        halo = prev_ref[:, cols]
        pieces.insert(0, jnp.where(first, jnp.zeros_like(halo), halo))
    if hi > m:
        halo = next_ref[:, cols]
        pieces.append(jnp.where(last, jnp.zeros_like(halo), halo))
    window = pieces[0] if len(pieces) == 1 else jnp.concatenate(pieces, axis=0)
    sh = _dot(shift, window)
    x = x_ref[r0:r0 + rows, cols].astype(F32)
    return (sh[:rows] * w_ref[0:1, cols] + x * w_ref[1:2, cols] + sh[rows:] * w_ref[2:3, cols]
            + b_ref[:, cols])


def _halo_specs(tm, width, rows_per_blk, n_rows):
    nb = tm // rows_per_blk
    last_blk = n_rows // rows_per_blk - 1
    prev = pl.BlockSpec((rows_per_blk, width), lambda i: (jnp.maximum(i * nb - 1, 0), 0))
    nxt = pl.BlockSpec((rows_per_blk, width), lambda i: (jnp.minimum((i + 1) * nb, last_blk), 0))
    return prev, nxt


def _mix_out_kernel(x_ref, yf_ref, yb_ref, r_ref, k_ref, v_ref, gd_ref, c_ref, cp_ref, cn_ref, bg_ref,
                    sgc_ref, sgr_ref, cw_ref, cb_ref, wa_ref, g2_ref, rk_ref, gnw_ref, gnb_ref,
                    bd_ref, wb_ref, wo_ref, g_ref, o_ref, *, tm, seq):
    i = pl.program_id(0)
    first = (i * tm) % seq == 0
    last = ((i + 1) * tm) % seq == 0
    rows = min(CONV_ROWS, tm)
    shift = _shift_matrix(rows)
    conv = jnp.concatenate([_conv3(c_ref, cp_ref, cn_ref, shift, cw_ref, cb_ref, slice(None), first,
                                   last, r0) for r0 in range(0, tm, rows)], axis=0)
    y_conv = _dot((bg_ref[...].astype(F32) * conv).astype(BF16), wa_ref[...])
    bd = bd_ref[...]
    y = yf_ref[...] + yb_ref[...]
    mean = _head_sum(y, bd) * (1.0 / HEAD)
    yc = y - mean
    var = _head_sum(yc * yc, bd) * (1.0 / HEAD)
    yn = yc * lax.rsqrt(var + GN_EPS) * gnw_ref[...] + gnb_ref[...]
    bonus = _head_sum(r_ref[...] * k_ref[...] * rk_ref[...], bd) * v_ref[...]
    gate = _dot(_sigmoid(gd_ref[...]).astype(BF16), g2_ref[...])
    y_rwkv = _dot(((yn + bonus) * gate).astype(BF16), wb_ref[...])
    merged = sgc_ref[...].astype(F32) * y_conv + sgr_ref[...].astype(F32) * y_rwkv
    m = _dot(merged.astype(BF16), wo_ref[...])
    o_ref[...] = x_ref[...] + _rms(m, g_ref[...])


def _mix_out(x, yf, yb, r, k, v, gd, cghc, bgate, sgc, sgr, conv_w, conv_b, w_a, g2, r_k, gn_w, gn_b,
             bd, w_b, w_out, g_post, *, tm, seq):
    n, d_model = x.shape
    d = r.shape[1]
    d_conv = cghc.shape[1]
    row = lambda w: pl.BlockSpec((tm, w), lambda i: (i, 0))
    cp, cn = _halo_specs(tm, d_conv, BF16_ROWS, n)
    return pl.pallas_call(
        functools.partial(_mix_out_kernel, tm=tm, seq=seq),
        grid=(n // tm,),
        in_specs=[row(d_model), row(d), row(d), row(d), row(d), row(d), row(gd.shape[1]),
                  row(d_conv), cp, cn, row(d_conv), row(d_model), row(d_model),
                  _full(conv_w.shape), _full(conv_b.shape), _full(w_a.shape), _full(g2.shape),
                  _full(r_k.shape), _full(gn_w.shape), _full(gn_b.shape), _full(bd.shape),
                  _full(w_b.shape), _full(w_out.shape), _full(g_post.shape)],
        out_specs=row(d_model),
        out_shape=jax.ShapeDtypeStruct((n, d_model), F32),
        compiler_params=_params(1),
        name="mix_out",
    )(x, yf, yb, r, k, v, gd, cghc, cghc, cghc, bgate, sgc, sgr, conv_w, conv_b, w_a, g2, r_k, gn_w,
      gn_b, bd, w_b, w_out, g_post)


def _ffn_up_kernel(x_ref, g_ref, w_ref, o_ref, *, col_chunk):
    u = _rms(x_ref[...], g_ref[...]).astype(BF16)
    for c0 in range(0, w_ref.shape[1], col_chunk):
        o_ref[:, c0:c0 + col_chunk] = _dot(u, w_ref[:, c0:c0 + col_chunk]).astype(BF16)


def _ffn_up(x, g, w_up, *, tm, col_chunk):
    n, d_model = x.shape
    cols = w_up.shape[1]
    return pl.pallas_call(
        functools.partial(_ffn_up_kernel, col_chunk=col_chunk),
        grid=(n // tm,),
        in_specs=[pl.BlockSpec((tm, d_model), lambda i: (i, 0)), _full((1, d_model)),
                  _full((d_model, cols))],
        out_specs=pl.BlockSpec((tm, cols), lambda i: (i, 0)),
        out_shape=jax.ShapeDtypeStruct((n, cols), BF16),
        compiler_params=_params(1),
        name="ffn_up",
    )(x, g, w_up)


def _ffn_down_kernel(x_ref, p_ref, h_ref, hp_ref, hn_ref, cw_ref, cb_ref, wd_ref, gf_ref, wp_ref, wg_ref,
                     gp_ref, o_ref, act_s, *, tm, seq, d_ff, col_chunk):
    i = pl.program_id(0)
    first = (i * tm) % seq == 0
    last = ((i + 1) * tm) % seq == 0
    rows = min(CONV_ROWS, tm)
    shift = _shift_matrix(rows)
    for c0 in range(0, d_ff, col_chunk):
        for r0 in range(0, tm, rows):
            hg, hv = [_conv3(h_ref, hp_ref, hn_ref, shift, cw_ref, cb_ref, slice(off, off + col_chunk),
                             first, last, r0) for off in (c0, d_ff + c0)]
            gelu = 0.5 * hg * (1.0 + jnp.tanh(GELU_C * (hg + 0.044715 * (hg * hg * hg))))
            act_s[r0:r0 + rows, c0:c0 + col_chunk] = (gelu * hv).astype(BF16)
    f = _dot(act_s[...], wd_ref[...])
    x = x_ref[...] + _rms(f, gf_ref[...])
    gate = _sigmoid(_dot(x.astype(BF16), wg_ref[...]))
    pe = _dot(p_ref[...].astype(BF16), wp_ref[...])
    o_ref[...] = x + _rms(gate * pe, gp_ref[...])


def _ffn_down(x, p, h, conv_w, conv_b, w_down, g_ffn, w_ple, w_gate, g_ple, *, tm, seq, col_chunk):
    n, d_model = x.shape
    d_ff = w_down.shape[0]
    row = lambda w: pl.BlockSpec((tm, w), lambda i: (i, 0))
    hp, hn = _halo_specs(tm, 2 * d_ff, BF16_ROWS, n)
    return pl.pallas_call(
        functools.partial(_ffn_down_kernel, tm=tm, seq=seq, d_ff=d_ff, col_chunk=col_chunk),
        grid=(n // tm,),
        in_specs=[row(d_model), row(p.shape[1]), row(2 * d_ff), hp, hn, _full(conv_w.shape),
                  _full(conv_b.shape), _full(w_down.shape), _full(g_ffn.shape), _full(w_ple.shape),
                  _full(w_gate.shape), _full(g_ple.shape)],
        out_specs=row(d_model),
        out_shape=jax.ShapeDtypeStruct((n, d_model), F32),
        scratch_shapes=[pltpu.VMEM((tm, d_ff), BF16)],
        compiler_params=_params(1),
        name="ffn_down",
    )(x, p, h, h, h, conv_w, conv_b, w_down, g_ffn, w_ple, w_gate, g_ple)


def _tiles(seq):
    tm = min(256, seq)
    tb = min(256, seq)
    return tm, tb


def _layer_weights(i, norm_mix_pre, norm_mix_post, norm_ffn_pre, norm_ffn_post, norm_ple_post, w_in,
                   conv_w, conv_b, w_branch_a, shift_mu, decay_w0, decay_w2, iclr_a0, iclr_a2, gate_g2,
                   k_k, k_a, r_k, gn_w, gn_b, w_branch_b, w_out, w_up, ffn_conv_w, ffn_conv_b, w_down,
                   w_ple, w_ple_gate):
    d_rwkv = k_k.shape[1]
    lora = decay_w2.shape[2]
    row = lambda a: a[i].reshape(1, -1)
    zeros = jnp.zeros((lora, d_rwkv), F32)
    lowrank = [jnp.concatenate([jnp.concatenate([decay_w2[i, d], zeros], axis=1),
                                jnp.concatenate([zeros, iclr_a2[i, d]], axis=1)], axis=0).astype(BF16)
               for d in range(2)]
    head_id = jnp.arange(d_rwkv) // HEAD
    return dict(
        g_mix_pre=row(norm_mix_pre), g_mix_post=row(norm_mix_post), g_ffn_pre=row(norm_ffn_pre),
        g_ffn_post=row(norm_ffn_post), g_ple_post=row(norm_ple_post),
        w_in=w_in[i].astype(BF16), conv_w=conv_w[i], conv_b=row(conv_b),
        w_a=w_branch_a[i].astype(BF16),
        mu=shift_mu[i][:, None, :], w0=decay_w0[i][:, None, :], a0=iclr_a0[i][:, None, :],
        lowrank=jnp.stack(lowrank), g2=gate_g2[i].astype(BF16), k_k=row(k_k), k_a=row(k_a),
        r_k=r_k[i].reshape(1, -1), gn_w=row(gn_w), gn_b=row(gn_b),
        bd=(head_id[:, None] == head_id[None, :]).astype(BF16),
        w_b=w_branch_b[i].astype(BF16), w_out=w_out[i].astype(BF16), w_up=w_up[i].astype(BF16),
        ffn_conv_w=ffn_conv_w[i], ffn_conv_b=row(ffn_conv_b), w_down=w_down[i].astype(BF16),
        w_ple=w_ple[i].astype(BF16), w_gate=w_ple_gate[i].astype(BF16))


def _layer(x, p, lw, *, bsz, seq):
    tm, tb = _tiles(seq)
    d_conv = lw["conv_w"].shape[1]
    d_rwkv = lw["k_k"].shape[1]
    d_z = lw["mu"].shape[2]
    d_g = lw["g2"].shape[0]
    d_ff = lw["w_down"].shape[0]
    cghc, bgate, r, k, v, z, gd, sgc, sgr = _in_proj(
        x, lw["g_mix_pre"], lw["w_in"], tm=tm, d_conv=d_conv, d_rwkv=d_rwkv, d_z=2 * d_z, d_g=d_g)
    ys = _wkv(r, k, v, z, lw["mu"], lw["w0"], lw["a0"], lw["lowrank"], lw["k_k"], lw["k_a"], lw["bd"],
              bsz=bsz, seq=seq, tb=tb)
    x = _mix_out(x, ys[0], ys[1], r, k, v, gd, cghc, bgate, sgc, sgr, lw["conv_w"], lw["conv_b"],
                 lw["w_a"], lw["g2"], lw["r_k"], lw["gn_w"], lw["gn_b"], lw["bd"], lw["w_b"],
                 lw["w_out"], lw["g_mix_post"], tm=tm, seq=seq)
    h = _ffn_up(x, lw["g_ffn_pre"], lw["w_up"], tm=tm, col_chunk=math.gcd(2 * d_ff, 512))
    return _ffn_down(x, p, h, lw["ffn_conv_w"], lw["ffn_conv_b"], lw["w_down"], lw["g_ffn_post"],
                     lw["w_ple"], lw["w_gate"], lw["g_ple_post"], tm=tm, seq=seq,
                     col_chunk=math.gcd(d_ff, 256))


def kernel(x_prompt, x_sample, p_prompt, p_sample, norm_mix_pre, norm_mix_post, norm_ffn_pre, norm_ffn_post, norm_ple_post, w_in, conv_w, conv_b, w_branch_a, shift_mu, decay_w0, decay_w2, iclr_a0, iclr_a2, gate_g2, k_k, k_a, r_k, gn_w, gn_b, w_branch_b, w_out, w_up, ffn_conv_w, ffn_conv_b, w_down, w_ple, w_ple_gate):
    weights = (norm_mix_pre, norm_mix_post, norm_ffn_pre, norm_ffn_post, norm_ple_post, w_in, conv_w,
               conv_b, w_branch_a, shift_mu, decay_w0, decay_w2, iclr_a0, iclr_a2, gate_g2, k_k, k_a,
               r_k, gn_w, gn_b, w_branch_b, w_out, w_up, ffn_conv_w, ffn_conv_b, w_down, w_ple,
               w_ple_gate)
    layers = [_layer_weights(i, *weights) for i in range(w_in.shape[0])]
    outs = []
    for x, p in ((x_prompt, p_prompt), (x_sample, p_sample)):
        bsz, seq, d_model = x.shape
        y = x.reshape(bsz * seq, d_model)
        for i, lw in enumerate(layers):
            y = _layer(y, p[i].reshape(bsz * seq, -1), lw, bsz=bsz, seq=seq)
        outs.append(y.reshape(bsz, seq, d_model))
    return tuple(outs)
```

```python
import functools
import math

import jax
import jax.numpy as jnp
from jax import lax
from jax.experimental import pallas as pl
from jax.experimental.pallas import tpu as pltpu

F32 = jnp.float32
BF16 = jnp.bfloat16

LANES = 128
SUBLANES = 8
BF16_ROWS = 16
VMEM_LIMIT_BYTES = 56 * 1024 * 1024

HEAD = 64
PAIR = 2 * HEAD
CHUNK = 64
CHUNKS_PER_STEP = 2
CONV_ROWS = 128
NORM_EPS = 1e-6
GN_EPS = HEAD * 1e-5
DECAY_SCALE = math.exp(-0.5)
GELU_C = math.sqrt(2.0 / math.pi)


def _sigmoid(x):
    return 1.0 / (1.0 + jnp.exp(-x))


def _rms(x, g):
    return x * lax.rsqrt(jnp.mean(x * x, axis=-1, keepdims=True) + NORM_EPS) * g


def _dot(a, b):
    return jnp.dot(a, b, preferred_element_type=F32)


def _dot_nt(a, b):
    return lax.dot_general(a, b, (((1,), (1,)), ((), ())), preferred_element_type=F32)


def _dot_tn(a, b):
    return lax.dot_general(a, b, (((0,), (0,)), ((), ())), preferred_element_type=F32)


def _split2(x):
    hi = x.astype(BF16)
    lo = (x - hi.astype(F32)).astype(BF16)
    return hi, lo


def _head_sum(x, bd):
    hi, lo = _split2(x)
    return _dot(hi, bd) + _dot(lo, bd)


def _params(n_axes):
    return pltpu.CompilerParams(dimension_semantics=("arbitrary",) * n_axes,
                                vmem_limit_bytes=VMEM_LIMIT_BYTES)


def _full(shape):
    nd = len(shape)
    return pl.BlockSpec(shape, lambda *_: (0,) * nd)


def _in_proj_kernel(x_ref, g_ref, w_ref, cghc_ref, bg_ref, r_ref, k_ref, v_ref, z_ref, gd_ref,
                    sgc_ref, sgr_ref, *, d_conv, d_rwkv, d_z, d_g, d_model):
    u = _rms(x_ref[...], g_ref[...]).astype(BF16)
    o = 0
    hbc = _dot(u, w_ref[:, o:o + 3 * d_conv])
    cghc_ref[...] = (hbc[:, 2 * d_conv:] * hbc[:, :d_conv]).astype(BF16)
    bg_ref[...] = hbc[:, d_conv:2 * d_conv].astype(BF16)
    o += 3 * d_conv
    for ref in (r_ref, k_ref, v_ref):
        ref[...] = _dot(u, w_ref[:, o:o + d_rwkv])
        o += d_rwkv
    z_ref[...] = _dot(u, w_ref[:, o:o + d_z])
    o += d_z
    gd_ref[...] = _dot(u, w_ref[:, o:o + d_g])
    o += d_g
    for ref in (sgc_ref, sgr_ref):
        ref[...] = _sigmoid(_dot(u, w_ref[:, o:o + d_model])).astype(BF16)
        o += d_model


def _in_proj(x, g, w_in, *, tm, d_conv, d_rwkv, d_z, d_g):
    n, d_model = x.shape
    cols = w_in.shape[1]
    row = lambda w: pl.BlockSpec((tm, w), lambda i: (i, 0))
    widths = (d_conv, d_conv, d_rwkv, d_rwkv, d_rwkv, d_z, d_g, d_model, d_model)
    dtypes = (BF16, BF16, F32, F32, F32, F32, F32, BF16, BF16)
    return pl.pallas_call(
        functools.partial(_in_proj_kernel, d_conv=d_conv, d_rwkv=d_rwkv, d_z=d_z, d_g=d_g,
                          d_model=d_model),
        grid=(n // tm,),
        in_specs=[row(d_model), _full((1, d_model)), _full((d_model, cols))],
        out_specs=[row(w) for w in widths],
        out_shape=[jax.ShapeDtypeStruct((n, w), dt) for w, dt in zip(widths, dtypes)],
        compiler_params=_params(1),
        name="in_proj",
    )(x, g, w_in)


def _wkv_masks(reverse):
    ti = lax.broadcasted_iota(jnp.int32, (CHUNK, PAIR), 0)
    si = lax.broadcasted_iota(jnp.int32, (CHUNK, PAIR), 1) & (HEAD - 1)
    strict = (si > ti) if reverse else (si < ti)
    incl = (si >= ti) if reverse else (si <= ti)
    li = lax.broadcasted_iota(jnp.int32, (CHUNK, CHUNK), 0)
    lj = lax.broadcasted_iota(jnp.int32, (CHUNK, CHUNK), 1)
    levels = []
    sz = 1
    while sz < CHUNK:
        same_blk = (ti & -(2 * sz)) == (si & -(2 * sz))
        t_hi, s_hi = (ti & sz) != 0, (si & sz) != 0
        lvl = (same_blk & ((~t_hi & s_hi) if reverse else (t_hi & ~s_hi))).astype(BF16)
        levels.append(lvl if sz == 1 else jnp.concatenate([lvl, lvl], axis=0))
        sz *= 2
    strict, incl = strict.astype(BF16), incl.astype(BF16)
    return dict(
        cum=((lj >= li) if reverse else (lj <= li)).astype(BF16),
        aa=jnp.concatenate([jnp.concatenate([strict, strict], axis=1),
                            jnp.concatenate([incl, incl], axis=1)], axis=0),
        eye=(si == ti).astype(F32), levels=levels, edge=0 if reverse else CHUNK - 1)


def _wkv_features(z_ref, mu, w2, zc_s, *, reverse, tb):
    z = z_ref[...]
    rows = lax.broadcasted_iota(jnp.int32, z.shape, 0)
    carry = jnp.broadcast_to(zc_s[0:1, :], z.shape)
    if reverse:
        zs = jnp.where(rows == tb - 1, carry, pltpu.roll(z, tb - 1, 0))
        zc_s[...] = jnp.broadcast_to(z[0:1, :], zc_s.shape)
    else:
        zs = jnp.where(rows == 0, carry, pltpu.roll(z, 1, 0))
        zc_s[...] = jnp.broadcast_to(z[tb - 1:tb, :], zc_s.shape)
    zm = z + mu * (zs - z)
    lanes = lax.broadcasted_iota(jnp.int32, z.shape, 1)
    feat = jnp.where(lanes < HEAD, jnp.tanh(zm), zm).astype(BF16)
    return _dot(feat, w2)


def _wkv_prep(lo, k, w0, a0, k_k, k_a, bd):
    d = k.shape[1]
    lw = -DECAY_SCALE * _sigmoid(w0 + lo[:, :d])
    a = _sigmoid(a0 + lo[:, d:])
    kkr = k * k_k
    kkn = kkr * lax.rsqrt(jnp.maximum(_head_sum(kkr * kkr, bd), 1e-24))
    return lw, kkn, kkn * a, k * (1.0 + (a - 1.0) * k_a)


def _wkv_kernel(rf_ref, kf_ref, vf_ref, zf_ref, rb_ref, kb_ref, vb_ref, zb_ref, mu_ref, w0_ref, a0_ref,
                w2_ref, kk_ref, ka_ref, bd_ref, yf_ref, yb_ref, state_s, zc_s, *, tb, n_pairs):
    nch = tb // CHUNK
    r_refs, k_refs, v_refs, z_refs, y_refs = ((rf_ref, rb_ref), (kf_ref, kb_ref), (vf_ref, vb_ref),
                                              (zf_ref, zb_ref), (yf_ref, yb_ref))

    @pl.when(pl.program_id(1) == 0)
    def _():
        state_s[...] = jnp.zeros_like(state_s)
        zc_s[...] = jnp.zeros_like(zc_s)

    lo = [_wkv_features(z_refs[dr], mu_ref[dr], w2_ref[dr], zc_s.at[dr], reverse=bool(dr), tb=tb)
          for dr in range(2)]
    masks = [_wkv_masks(False), _wkv_masks(True)]
    head_lo = (lax.broadcasted_iota(jnp.int32, (CHUNK, PAIR), 1) < HEAD).astype(BF16)
    head_hi = 1.0 - head_lo
    bi = lax.broadcasted_iota(jnp.int32, (PAIR, PAIR), 0)
    bj = lax.broadcasted_iota(jnp.int32, (PAIR, PAIR), 1)
    same_head = (bi < HEAD) == (bj < HEAD)

    def stack(x):
        xb = x.astype(BF16)
        return jnp.concatenate([xb * head_lo, xb * head_hi], axis=0)

    group = math.gcd(nch, CHUNKS_PER_STEP)
    span = group * CHUNK
    n_groups = nch // group
    units = [(dr, p, j) for j in range(group) for dr in range(2) for p in range(n_pairs)]
    halves = (units[:len(units) // 2], units[len(units) // 2:])
    n_levels = len(masks[0]["levels"])
    state = [state_s[i] for i in range(2 * n_pairs)]

    def rows(g, dr, j):
        base = g * span if dr == 0 else tb - (g + 1) * span
        off = base + (j if dr == 0 else group - 1 - j) * CHUNK
        return slice(off, off + CHUNK)

    def head(g, out):
        prep = {}
        for dr in range(2):
            blk = slice(rows(g, dr, 0 if dr == 0 else group - 1).start,
                        rows(g, dr, group - 1 if dr == 0 else 0).stop)
            prep[dr] = _wkv_prep(lo[dr][blk], k_refs[dr][blk, :], w0_ref[dr], a0_ref[dr], kk_ref[...],
                                 ka_ref[...], bd_ref[...])
            yield

        def take(which, dr, p, j):
            off = (j if dr == 0 else group - 1 - j) * CHUNK
            return prep[dr][which][off:off + CHUNK, p * PAIR:(p + 1) * PAIR]

        def load(refs, dr, p, j):
            return refs[dr][rows(g, dr, j), p * PAIR:(p + 1) * PAIR]

        for key in ("lw", "cw", "e_pos", "rw", "kkw", "binv", "kinv", "v", "a_rb", "a_kv", "tinv", "a_st",
                    "w_end", "bk_end"):
            out[key] = {}
        for part in halves:
            for un in part:
                dr, p, j = un
                out["lw"][un] = take(0, dr, p, j)
                h1, h2 = _split2(out["lw"][un])
                cs = _dot(masks[dr]["cum"], jnp.concatenate([h1, h2], axis=1))
                out["cw"][un] = cs[:, :PAIR] + cs[:, PAIR:]
            yield
        for part in halves:
            for un in part:
                dr, p, j = un
                cw = out["cw"][un]
                e_pos, e_neg = jnp.exp(cw), jnp.exp(-cw)
                out["e_pos"][un] = e_pos
                out["rw"][un] = load(r_refs, dr, p, j) * e_pos
                out["kkw"][un] = take(1, dr, p, j) * jnp.exp(cw - out["lw"][un])
                out["binv"][un] = take(2, dr, p, j) * e_neg
                out["kinv"][un] = take(3, dr, p, j) * e_neg
                out["v"][un] = load(v_refs, dr, p, j)
            yield
        for part in halves:
            for un in part:
                dr, p, j = un
                aa = (_dot_nt(jnp.concatenate([out["kkw"][un], out["rw"][un]], axis=0).astype(BF16),
                              jnp.concatenate([stack(out["binv"][un]), stack(out["kinv"][un])], axis=0)
                              ).astype(BF16) * masks[dr]["aa"])
                a_ab = aa[:CHUNK, :PAIR]
                out["a_rb"][un] = aa[CHUNK:, :PAIR]
                out["a_kv"][un] = aa[:, PAIR:]
                out["tinv"][un] = masks[dr]["eye"] - (a_ab * masks[dr]["levels"][0]).astype(F32)
                out["a_st"][un] = jnp.concatenate([a_ab * head_lo, a_ab * head_hi], axis=0)
                w_end = out["e_pos"][un][masks[dr]["edge"]:masks[dr]["edge"] + 1, :]
                out["w_end"][un] = w_end
                out["bk_end"][un] = jnp.concatenate([out["binv"][un] * w_end, out["kinv"][un] * w_end],
                                                    axis=0).astype(BF16)
            yield

    def tail(g, h):
        nonlocal state
        av, tt = {}, {}
        for part in halves:
            for un in part:
                av[un] = _dot(h["a_kv"][un], stack(h["v"][un]))
            yield
        for part in halves:
            for un in part:
                tt[un] = _dot(h["tinv"][un].astype(BF16),
                              jnp.concatenate([stack(h["kkw"][un]), stack(av[un][:CHUNK])], axis=1))
            yield
        for jj in range(group):
            uns = [un for un in units if un[2] == jj]
            ps = [_dot_nt(jnp.concatenate([tt[un][:, :PAIR], h["rw"][un]], axis=0).astype(BF16),
                          s.astype(BF16)) for un, s in zip(uns, state)]
            yield
            u = [-(x[:CHUNK] + tt[un][:, PAIR:]) for un, x in zip(uns, ps)]
            y = [x[CHUNK:] + _dot(h["a_rb"][un], stack(w)) + av[un][CHUNK:]
                 for un, x, w in zip(uns, ps, u)]
            upd = [_dot_tn(jnp.concatenate([x, h["v"][un]], axis=0).astype(BF16), h["bk_end"][un])
                   for un, x in zip(uns, u)]
            yield
            for (dr, p, j), x in zip(uns, y):
                y_refs[dr][rows(g, dr, j), p * PAIR:(p + 1) * PAIR] = x
            state = [s * h["w_end"][un] + jnp.where(same_head, x, 0.0)
                     for un, s, x in zip(uns, state, upd)]
            yield

    def advance(gens):
        for gen in gens:
            next(gen, None)

    def drain(gens):
        for gen in gens:
            for _ in gen:
                pass

    heads = [dict() for _ in range(n_groups)]
    drain([head(0, heads[0])])
    side = []
    for g in range(n_groups):
        h = heads[g]
        if g + 1 < n_groups:
            side.append(head(g + 1, heads[g + 1]))
        for lvl in range(1, n_levels):
            m1 = {un: _dot(h["tinv"][un].astype(BF16), h["a_st"][un] * masks[un[0]]["levels"][lvl])
                  for un in units}
            advance(side)
            h["tinv"] = {un: h["tinv"][un] - _dot(m1[un].astype(BF16), stack(h["tinv"][un]))
                         for un in units}
            advance(side)
        drain(side)
        side = [tail(g, h)]
    drain(side)
    for i, s in enumerate(state):
        state_s[i] = s


def _wkv(r, k, v, z, mu, w0, a0, w2, k_k, k_a, bd, *, bsz, seq, tb):
    n, d = r.shape
    nt = seq // tb
    n_pairs = d // PAIR
    fmap = lambda b, i: (b * nt + i, 0)
    bmap = lambda b, i: (b * nt + nt - 1 - i, 0)
    fblk, bblk = pl.BlockSpec((tb, d), fmap), pl.BlockSpec((tb, d), bmap)
    zf = pl.BlockSpec((tb, PAIR), fmap)
    zb = pl.BlockSpec((tb, PAIR), lambda b, i: (bmap(b, i)[0], 1))
    return pl.pallas_call(
        functools.partial(_wkv_kernel, tb=tb, n_pairs=n_pairs),
        grid=(bsz, nt),
        in_specs=[fblk, fblk, fblk, zf, bblk, bblk, bblk, zb,
                  _full(mu.shape), _full(w0.shape), _full(a0.shape), _full(w2.shape),
                  _full(k_k.shape), _full(k_a.shape), _full(bd.shape)],
        out_specs=[fblk, bblk],
        out_shape=[jax.ShapeDtypeStruct((n, d), F32)] * 2,
        scratch_shapes=[pltpu.VMEM((2 * n_pairs, PAIR, PAIR), F32), pltpu.VMEM((2, SUBLANES, PAIR), F32)],
        compiler_params=_params(2),
        name="wkv",
    )(r, k, v, z, r, k, v, z, mu, w0, a0, w2, k_k, k_a, bd)


def _shift_matrix(rows):
    shape = (2 * rows, rows + 2 * BF16_ROWS)
    r = lax.broadcasted_iota(jnp.int32, shape, 0)
    c = lax.broadcasted_iota(jnp.int32, shape, 1)
    target = jnp.where(r < rows, r + (BF16_ROWS - 1), r - rows + (BF16_ROWS + 1))
    return (c == target).astype(BF16)


def _conv3(x_ref, prev_ref, next_ref, shift, w_ref, b_ref, cols, first, last, r0):
    m = x_ref.shape[0]
    rows = shift.shape[0] // 2
    lo, hi = r0 - BF16_ROWS, r0 + rows + BF16_ROWS
    pieces = [x_ref[max(lo, 0):min(hi, m), cols]]
    if lo < 0:
        halo = prev_ref[:, cols]
        pieces.insert(0, jnp.where(first, jnp.zeros_like(halo), halo))
    if hi > m:
        halo = next_ref[:, cols]
        pieces.append(jnp.where(last, jnp.zeros_like(halo), halo))
    window = pieces[0] if len(pieces) == 1 else jnp.concatenate(pieces, axis=0)
    sh = _dot(shift, window)
    x = x_ref[r0:r0 + rows, cols].astype(F32)
    return (sh[:rows] * w_ref[0:1, cols] + x * w_ref[1:2, cols] + sh[rows:] * w_ref[2:3, cols]
            + b_ref[:, cols])


def _halo_specs(tm, width, rows_per_blk, n_rows):
    nb = tm // rows_per_blk
    last_blk = n_rows // rows_per_blk - 1
    prev = pl.BlockSpec((rows_per_blk, width), lambda i: (jnp.maximum(i * nb - 1, 0), 0))
    nxt = pl.BlockSpec((rows_per_blk, width), lambda i: (jnp.minimum((i + 1) * nb, last_blk), 0))
    return prev, nxt


def _mix_out_kernel(x_ref, yf_ref, yb_ref, r_ref, k_ref, v_ref, gd_ref, c_ref, cp_ref, cn_ref, bg_ref,
                    sgc_ref, sgr_ref, cw_ref, cb_ref, wa_ref, g2_ref, rk_ref, gnw_ref, gnb_ref,
                    bd_ref, wb_ref, wo_ref, g_ref, o_ref, *, tm, seq):
    i = pl.program_id(0)
    first = (i * tm) % seq == 0
    last = ((i + 1) * tm) % seq == 0
    rows = min(CONV_ROWS, tm)
    shift = _shift_matrix(rows)
    conv = jnp.concatenate([_conv3(c_ref, cp_ref, cn_ref, shift, cw_ref, cb_ref, slice(None), first,
                                   last, r0) for r0 in range(0, tm, rows)], axis=0)
    y_conv = _dot((bg_ref[...].astype(F32) * conv).astype(BF16), wa_ref[...])
    bd = bd_ref[...]
    y = yf_ref[...] + yb_ref[...]
    mean = _head_sum(y, bd) * (1.0 / HEAD)
    yc = y - mean
    var = _head_sum(yc * yc, bd) * (1.0 / HEAD)
    yn = yc * lax.rsqrt(var + GN_EPS) * gnw_ref[...] + gnb_ref[...]
    bonus = _head_sum(r_ref[...] * k_ref[...] * rk_ref[...], bd) * v_ref[...]
    gate = _dot(_sigmoid(gd_ref[...]).astype(BF16), g2_ref[...])
    y_rwkv = _dot(((yn + bonus) * gate).astype(BF16), wb_ref[...])
    merged = sgc_ref[...].astype(F32) * y_conv + sgr_ref[...].astype(F32) * y_rwkv
    m = _dot(merged.astype(BF16), wo_ref[...])
    o_ref[...] = x_ref[...] + _rms(m, g_ref[...])


def _mix_out(x, yf, yb, r, k, v, gd, cghc, bgate, sgc, sgr, conv_w, conv_b, w_a, g2, r_k, gn_w, gn_b,
             bd, w_b, w_out, g_post, *, tm, seq):
    n, d_model = x.shape
    d = r.shape[1]
    d_conv = cghc.shape[1]
    row = lambda w: pl.BlockSpec((tm, w), lambda i: (i, 0))
    cp, cn = _halo_specs(tm, d_conv, BF16_ROWS, n)
    return pl.pallas_call(
        functools.partial(_mix_out_kernel, tm=tm, seq=seq),
        grid=(n // tm,),
        in_specs=[row(d_model), row(d), row(d), row(d), row(d), row(d), row(gd.shape[1]),
                  row(d_conv), cp, cn, row(d_conv), row(d_model), row(d_model),
                  _full(conv_w.shape), _full(conv_b.shape), _full(w_a.shape), _full(g2.shape),
                  _full(r_k.shape), _full(gn_w.shape), _full(gn_b.shape), _full(bd.shape),
                  _full(w_b.shape), _full(w_out.shape), _full(g_post.shape)],
        out_specs=row(d_model),
        out_shape=jax.ShapeDtypeStruct((n, d_model), F32),
        compiler_params=_params(1),
        name="mix_out",
    )(x, yf, yb, r, k, v, gd, cghc, cghc, cghc, bgate, sgc, sgr, conv_w, conv_b, w_a, g2, r_k, gn_w,
      gn_b, bd, w_b, w_out, g_post)


def _ffn_kernel(x_ref, xp_ref, xn_ref, p_ref, gpre_ref, wu_ref, cw_ref, cb_ref, wd_ref, gf_ref, wp_ref,
                wg_ref, gp_ref, o_ref, act_s, *, tm, seq, d_ff, col_chunk):
    i = pl.program_id(0)
    first = (i * tm) % seq == 0
    last = ((i + 1) * tm) % seq == 0
    g = gpre_ref[...]
    u_prev = jnp.where(first, 0.0, _rms(xp_ref[...], g))
    u_next = jnp.where(last, 0.0, _rms(xn_ref[...], g))
    u = jnp.concatenate([u_prev, _rms(x_ref[...], g), u_next], axis=0).astype(BF16)
    m = tm + 2 * SUBLANES
    for c0 in range(0, d_ff, col_chunk):
        conv = []
        for off in (c0, d_ff + c0):
            cols = slice(off, off + col_chunk)
            h = _dot(u, wu_ref[:, cols])
            conv.append(pltpu.roll(h, 1, 0) * cw_ref[0:1, cols] + h * cw_ref[1:2, cols]
                        + pltpu.roll(h, m - 1, 0) * cw_ref[2:3, cols] + cb_ref[:, cols])
        hg, hv = conv
        gelu = 0.5 * hg * (1.0 + jnp.tanh(GELU_C * (hg + 0.044715 * (hg * hg * hg))))
        act_s[:, c0:c0 + col_chunk] = (gelu * hv)[SUBLANES:SUBLANES + tm].astype(BF16)
    f = _dot(act_s[...], wd_ref[...])
    x = x_ref[...] + _rms(f, gf_ref[...])
    gate = _sigmoid(_dot(x.astype(BF16), wg_ref[...]))
    pe = _dot(p_ref[...].astype(BF16), wp_ref[...])
    o_ref[...] = x + _rms(gate * pe, gp_ref[...])


def _ffn(x, p, g_pre, w_up, conv_w, conv_b, w_down, g_ffn, w_ple, w_gate, g_ple, *, tm, seq, col_chunk):
    n, d_model = x.shape
    d_ff = w_down.shape[0]
    row = lambda w: pl.BlockSpec((tm, w), lambda i: (i, 0))
    xp, xn = _halo_specs(tm, d_model, SUBLANES, n)
    const = lambda a: pl.BlockSpec(a.shape, lambda i: (0,) * a.ndim, pipeline_mode=pl.Buffered(1))
    return pl.pallas_call(
        functools.partial(_ffn_kernel, tm=tm, seq=seq, d_ff=d_ff, col_chunk=col_chunk),
        grid=(n // tm,),
        in_specs=[row(d_model), xp, xn, row(p.shape[1]), const(g_pre), const(w_up), const(conv_w),
                  const(conv_b), const(w_down), const(g_ffn), const(w_ple), const(w_gate), const(g_ple)],
        out_specs=row(d_model),
        out_shape=jax.ShapeDtypeStruct((n, d_model), F32),
        scratch_shapes=[pltpu.VMEM((tm, d_ff), BF16)],
        compiler_params=_params(1),
        name="ffn",
    )(x, x, x, p, g_pre, w_up, conv_w, conv_b, w_down, g_ffn, w_ple, w_gate, g_ple)


def _tiles(seq):
    tm = min(256, seq)
    tb = min(512, seq)
    return tm, tb


def _layer_weights(i, norm_mix_pre, norm_mix_post, norm_ffn_pre, norm_ffn_post, norm_ple_post, w_in,
                   conv_w, conv_b, w_branch_a, shift_mu, decay_w0, decay_w2, iclr_a0, iclr_a2, gate_g2,
                   k_k, k_a, r_k, gn_w, gn_b, w_branch_b, w_out, w_up, ffn_conv_w, ffn_conv_b, w_down,
                   w_ple, w_ple_gate):
    d_rwkv = k_k.shape[1]
    lora = decay_w2.shape[2]
    row = lambda a: a[i].reshape(1, -1)
    zeros = jnp.zeros((lora, d_rwkv), F32)
    lowrank = [jnp.concatenate([jnp.concatenate([decay_w2[i, d], zeros], axis=1),
                                jnp.concatenate([zeros, iclr_a2[i, d]], axis=1)], axis=0).astype(BF16)
               for d in range(2)]
    head_id = jnp.arange(d_rwkv) // HEAD
    return dict(
        g_mix_pre=row(norm_mix_pre), g_mix_post=row(norm_mix_post), g_ffn_pre=row(norm_ffn_pre),
        g_ffn_post=row(norm_ffn_post), g_ple_post=row(norm_ple_post),
        w_in=w_in[i].astype(BF16), conv_w=conv_w[i], conv_b=row(conv_b),
        w_a=w_branch_a[i].astype(BF16),
        mu=shift_mu[i][:, None, :], w0=decay_w0[i][:, None, :], a0=iclr_a0[i][:, None, :],
        lowrank=jnp.stack(lowrank), g2=gate_g2[i].astype(BF16), k_k=row(k_k), k_a=row(k_a),
        r_k=r_k[i].reshape(1, -1), gn_w=row(gn_w), gn_b=row(gn_b),
        bd=(head_id[:, None] == head_id[None, :]).astype(BF16),
        w_b=w_branch_b[i].astype(BF16), w_out=w_out[i].astype(BF16), w_up=w_up[i].astype(BF16),
        ffn_conv_w=ffn_conv_w[i], ffn_conv_b=row(ffn_conv_b), w_down=w_down[i].astype(BF16),
        w_ple=w_ple[i].astype(BF16), w_gate=w_ple_gate[i].astype(BF16))


def _layer(x, p, lw, *, bsz, seq):
    tm, tb = _tiles(seq)
    d_conv = lw["conv_w"].shape[1]
    d_rwkv = lw["k_k"].shape[1]
    d_z = lw["mu"].shape[2]
    d_g = lw["g2"].shape[0]
    d_ff = lw["w_down"].shape[0]
    cghc, bgate, r, k, v, z, gd, sgc, sgr = _in_proj(
        x, lw["g_mix_pre"], lw["w_in"], tm=tm, d_conv=d_conv, d_rwkv=d_rwkv, d_z=2 * d_z, d_g=d_g)
    ys = _wkv(r, k, v, z, lw["mu"], lw["w0"], lw["a0"], lw["lowrank"], lw["k_k"], lw["k_a"], lw["bd"],
              bsz=bsz, seq=seq, tb=tb)
    x = _mix_out(x, ys[0], ys[1], r, k, v, gd, cghc, bgate, sgc, sgr, lw["conv_w"], lw["conv_b"],
                 lw["w_a"], lw["g2"], lw["r_k"], lw["gn_w"], lw["gn_b"], lw["bd"], lw["w_b"],
                 lw["w_out"], lw["g_mix_post"], tm=tm, seq=seq)
    return _ffn(x, p, lw["g_ffn_pre"], lw["w_up"], lw["ffn_conv_w"], lw["ffn_conv_b"], lw["w_down"],
                lw["g_ffn_post"], lw["w_ple"], lw["w_gate"], lw["g_ple_post"], tm=tm, seq=seq,
                col_chunk=math.gcd(d_ff, 256))


def kernel(x_prompt, x_sample, p_prompt, p_sample, norm_mix_pre, norm_mix_post, norm_ffn_pre, norm_ffn_post, norm_ple_post, w_in, conv_w, conv_b, w_branch_a, shift_mu, decay_w0, decay_w2, iclr_a0, iclr_a2, gate_g2, k_k, k_a, r_k, gn_w, gn_b, w_branch_b, w_out, w_up, ffn_conv_w, ffn_conv_b, w_down, w_ple, w_ple_gate):
    weights = (norm_mix_pre, norm_mix_post, norm_ffn_pre, norm_ffn_post, norm_ple_post, w_in, conv_w,
               conv_b, w_branch_a, shift_mu, decay_w0, decay_w2, iclr_a0, iclr_a2, gate_g2, k_k, k_a,
               r_k, gn_w, gn_b, w_branch_b, w_out, w_up, ffn_conv_w, ffn_conv_b, w_down, w_ple,
               w_ple_gate)
    layers = [_layer_weights(i, *weights) for i in range(w_in.shape[0])]
    outs = []
    for x, p in ((x_prompt, p_prompt), (x_sample, p_sample)):
        bsz, seq, d_model = x.shape
        y = x.reshape(bsz * seq, d_model)
        for i, lw in enumerate(layers):
            y = _layer(y, p[i].reshape(bsz * seq, -1), lw, bsz=bsz, seq=seq)
        outs.append(y.reshape(bsz, seq, d_model))
    return tuple(outs)
```

```python
import functools
import math

import jax
import jax.numpy as jnp
from jax import lax
from jax.experimental import pallas as pl
from jax.experimental.pallas import tpu as pltpu

F32 = jnp.float32
BF16 = jnp.bfloat16

LANES = 128
SUBLANES = 8
BF16_ROWS = 16
VMEM_LIMIT_BYTES = 56 * 1024 * 1024

HEAD = 64
PAIR = 2 * HEAD
CHUNK = 64
CHUNKS_PER_STEP = 2
CONV_ROWS = 128
NORM_EPS = 1e-6
GN_EPS = HEAD * 1e-5
DECAY_SCALE = math.exp(-0.5)
GELU_C = math.sqrt(2.0 / math.pi)


def _sigmoid(x):
    return 1.0 / (1.0 + jnp.exp(-x))


def _rms(x, g):
    return x * lax.rsqrt(jnp.mean(x * x, axis=-1, keepdims=True) + NORM_EPS) * g


def _dot(a, b):
    return jnp.dot(a, b, preferred_element_type=F32)


def _dot_nt(a, b):
    return lax.dot_general(a, b, (((1,), (1,)), ((), ())), preferred_element_type=F32)


def _dot_tn(a, b):
    return lax.dot_general(a, b, (((0,), (0,)), ((), ())), preferred_element_type=F32)


def _split2(x):
    hi = x.astype(BF16)
    lo = (x - hi.astype(F32)).astype(BF16)
    return hi, lo


def _head_sum(x, bd):
    return _dot(x.astype(BF16), bd)


def _params(n_axes):
    return pltpu.CompilerParams(dimension_semantics=("arbitrary",) * n_axes,
                                vmem_limit_bytes=VMEM_LIMIT_BYTES)


def _full(shape):
    nd = len(shape)
    return pl.BlockSpec(shape, lambda *_: (0,) * nd)


def _in_proj_kernel(x_ref, g_ref, w_ref, cghc_ref, bg_ref, r_ref, k_ref, v_ref, z_ref, gd_ref,
                    sgc_ref, sgr_ref, *, d_conv, d_rwkv, d_z, d_g, d_model):
    u = _rms(x_ref[...], g_ref[...]).astype(BF16)
    o = 0
    hbc = _dot(u, w_ref[:, o:o + 3 * d_conv])
    cghc_ref[...] = (hbc[:, 2 * d_conv:] * hbc[:, :d_conv]).astype(BF16)
    bg_ref[...] = hbc[:, d_conv:2 * d_conv].astype(BF16)
    o += 3 * d_conv
    for ref in (r_ref, k_ref, v_ref):
        ref[...] = _dot(u, w_ref[:, o:o + d_rwkv])
        o += d_rwkv
    z_ref[...] = _dot(u, w_ref[:, o:o + d_z])
    o += d_z
    gd_ref[...] = _dot(u, w_ref[:, o:o + d_g])
    o += d_g
    for ref in (sgc_ref, sgr_ref):
        ref[...] = _sigmoid(_dot(u, w_ref[:, o:o + d_model])).astype(BF16)
        o += d_model


def _in_proj(x, g, w_in, *, tm, d_conv, d_rwkv, d_z, d_g):
    n, d_model = x.shape
    cols = w_in.shape[1]
    row = lambda w: pl.BlockSpec((tm, w), lambda i: (i, 0))
    widths = (d_conv, d_conv, d_rwkv, d_rwkv, d_rwkv, d_z, d_g, d_model, d_model)
    dtypes = (BF16, BF16, F32, F32, F32, F32, F32, BF16, BF16)
    return pl.pallas_call(
        functools.partial(_in_proj_kernel, d_conv=d_conv, d_rwkv=d_rwkv, d_z=d_z, d_g=d_g,
                          d_model=d_model),
        grid=(n // tm,),
        in_specs=[row(d_model), _full((1, d_model)), _full((d_model, cols))],
        out_specs=[row(w) for w in widths],
        out_shape=[jax.ShapeDtypeStruct((n, w), dt) for w, dt in zip(widths, dtypes)],
        compiler_params=_params(1),
        name="in_proj",
    )(x, g, w_in)


def _wkv_masks():
    ti = lax.broadcasted_iota(jnp.int32, (CHUNK, PAIR), 0)
    lane = lax.broadcasted_iota(jnp.int32, (CHUNK, PAIR), 1)
    si = lane & (HEAD - 1)
    li = lax.broadcasted_iota(jnp.int32, (CHUNK, CHUNK), 0)
    lj = lax.broadcasted_iota(jnp.int32, (CHUNK, CHUNK), 1)
    head_lo = (lane < HEAD).astype(BF16)
    cum, aa, lvl0, lvls = [], [], [], []
    for reverse in (False, True):
        strict = ((si > ti) if reverse else (si < ti)).astype(BF16)
        incl = ((si >= ti) if reverse else (si <= ti)).astype(BF16)
        cum.append(((lj >= li) if reverse else (lj <= li)).astype(BF16))
        aa.append(jnp.concatenate([jnp.concatenate([strict, strict], axis=1),
                                   jnp.concatenate([incl, incl], axis=1)], axis=0))
        levels = []
        sz = 1
        while sz < CHUNK:
            same_blk = (ti & -(2 * sz)) == (si & -(2 * sz))
            t_hi, s_hi = (ti & sz) != 0, (si & sz) != 0
            levels.append((same_blk & ((~t_hi & s_hi) if reverse else (t_hi & ~s_hi))).astype(BF16))
            sz *= 2
        lvl0.append(levels[0])
        lvls.append(jnp.stack([jnp.concatenate([m * head_lo, m * (1 - head_lo)], axis=0)
                               for m in levels[1:]]))
    bi = lax.broadcasted_iota(jnp.int32, (PAIR, PAIR), 0)
    bj = lax.broadcasted_iota(jnp.int32, (PAIR, PAIR), 1)
    return dict(cum=jnp.stack(cum), aa=jnp.stack(aa), lvl0=jnp.stack(lvl0), lvls=jnp.stack(lvls),
                eye=(si == ti).astype(BF16), heads=jnp.stack([head_lo, 1 - head_lo]),
                same_head=((bi < HEAD) == (bj < HEAD)).astype(F32))


def _wkv_features(z_ref, mu, w2, zc_s, *, reverse, tb):
    z = z_ref[...]
    rows = lax.broadcasted_iota(jnp.int32, z.shape, 0)
    carry = jnp.broadcast_to(zc_s[0:1, :], z.shape)
    if reverse:
        zs = jnp.where(rows == tb - 1, carry, pltpu.roll(z, tb - 1, 0))
        zc_s[...] = jnp.broadcast_to(z[0:1, :], zc_s.shape)
    else:
        zs = jnp.where(rows == 0, carry, pltpu.roll(z, 1, 0))
        zc_s[...] = jnp.broadcast_to(z[tb - 1:tb, :], zc_s.shape)
    zm = z + mu * (zs - z)
    lanes = lax.broadcasted_iota(jnp.int32, z.shape, 1)
    feat = jnp.where(lanes < HEAD, jnp.tanh(zm), zm).astype(BF16)
    return _dot(feat, w2)


def _wkv_prep(lo, k, w0, a0, k_k, k_a, bd):
    d = k.shape[1]
    lw = -DECAY_SCALE * _sigmoid(w0 + lo[:, :d])
    a = _sigmoid(a0 + lo[:, d:])
    kkr = k * k_k
    kkn = kkr * lax.rsqrt(jnp.maximum(_head_sum(kkr * kkr, bd), 1e-24))
    return lw, kkn, kkn * a, k * (1.0 + (a - 1.0) * k_a)


def _wkv_kernel(rf_ref, kf_ref, vf_ref, zf_ref, rb_ref, kb_ref, vb_ref, zb_ref, mu_ref, w0_ref, a0_ref,
                w2_ref, kk_ref, ka_ref, bd_ref, cum_ref, aa_ref, lvl0_ref, lvls_ref, eye_ref, heads_ref,
                same_ref, yf_ref, yb_ref, state_s, zc_s, *, tb, n_pairs):
    nch = tb // CHUNK
    r_refs, k_refs, v_refs, z_refs, y_refs = ((rf_ref, rb_ref), (kf_ref, kb_ref), (vf_ref, vb_ref),
                                              (zf_ref, zb_ref), (yf_ref, yb_ref))

    @pl.when(pl.program_id(1) == 0)
    def _():
        state_s[...] = jnp.zeros_like(state_s)
        zc_s[...] = jnp.zeros_like(zc_s)

    lo = [_wkv_features(z_refs[dr], mu_ref[dr], w2_ref[dr], zc_s.at[dr], reverse=bool(dr), tb=tb)
          for dr in range(2)]
    edge = (CHUNK - 1, 0)

    def stack(x):
        xb = x.astype(BF16)
        return jnp.concatenate([xb * heads_ref[0], xb * heads_ref[1]], axis=0)

    group = math.gcd(nch, CHUNKS_PER_STEP)
    span = group * CHUNK
    n_groups = nch // group
    units = [(dr, p, j) for j in range(group) for dr in range(2) for p in range(n_pairs)]
    halves = (units[:len(units) // 2], units[len(units) // 2:])
    n_levels = lvls_ref.shape[1] + 1
    state = [state_s[i] for i in range(2 * n_pairs)]

    def rows(g, dr, j):
        base = g * span if dr == 0 else tb - (g + 1) * span
        off = base + (j if dr == 0 else group - 1 - j) * CHUNK
        return slice(off, off + CHUNK)

    def head(g, out):
        prep = {}
        for dr in range(2):
            blk = slice(rows(g, dr, 0 if dr == 0 else group - 1).start,
                        rows(g, dr, group - 1 if dr == 0 else 0).stop)
            prep[dr] = _wkv_prep(lo[dr][blk], k_refs[dr][blk, :], w0_ref[dr], a0_ref[dr], kk_ref[...],
                                 ka_ref[...], bd_ref[...])
            yield

        def take(which, dr, p, j):
            off = (j if dr == 0 else group - 1 - j) * CHUNK
            return prep[dr][which][off:off + CHUNK, p * PAIR:(p + 1) * PAIR]

        def load(refs, dr, p, j):
            return refs[dr][rows(g, dr, j), p * PAIR:(p + 1) * PAIR]

        for key in ("cw", "rw", "kkw", "v", "a_ab", "a_rb", "a_kv", "tinv", "w_end", "bk_end"):
            out[key] = {}
        for part in halves:
            for un in part:
                dr, p, j = un
                h1, h2 = _split2(take(0, dr, p, j))
                cs = _dot(cum_ref[dr], jnp.concatenate([h1, h2], axis=1))
                out["cw"][un] = cs[:, :PAIR] + cs[:, PAIR:]
            yield
        for part in halves:
            for un in part:
                dr, p, j = un
                cw = out["cw"].pop(un)
                e_pos, e_neg = jnp.exp(cw), jnp.exp(-cw)
                rw = (load(r_refs, dr, p, j) * e_pos).astype(BF16)
                kkw = (take(1, dr, p, j) * jnp.exp(cw - take(0, dr, p, j))).astype(BF16)
                binv = take(2, dr, p, j) * e_neg
                kinv = take(3, dr, p, j) * e_neg
                w_end = e_pos[edge[dr]:edge[dr] + 1, :]
                out["rw"][un], out["kkw"][un], out["w_end"][un] = rw, kkw, w_end
                out["v"][un] = load(v_refs, dr, p, j).astype(BF16)
                out["bk_end"][un] = jnp.concatenate([binv * w_end, kinv * w_end], axis=0).astype(BF16)
                aa = (_dot_nt(jnp.concatenate([kkw, rw], axis=0),
                              jnp.concatenate([stack(binv), stack(kinv)], axis=0)).astype(BF16)
                      * aa_ref[dr])
                a_ab = aa[:CHUNK, :PAIR]
                out["a_ab"][un] = a_ab
                out["a_rb"][un] = aa[CHUNK:, :PAIR]
                out["a_kv"][un] = aa[:, PAIR:]
                out["tinv"][un] = eye_ref[...] - a_ab * lvl0_ref[dr]
            yield

    def tail(g, h):
        nonlocal state
        av, tt = {}, {}
        for part in halves:
            for un in part:
                av[un] = _dot(h["a_kv"][un], stack(h["v"][un]))
            yield
        for part in halves:
            for un in part:
                tt[un] = _dot(h["tinv"][un],
                              jnp.concatenate([stack(h["kkw"][un]), stack(av[un][:CHUNK])], axis=1))
            yield
        for jj in range(group):
            uns = [un for un in units if un[2] == jj]
            ps = [_dot_nt(jnp.concatenate([tt[un][:, :PAIR].astype(BF16), h["rw"][un]], axis=0),
                          s.astype(BF16)) for un, s in zip(uns, state)]
            yield
            u = [-(x[:CHUNK] + tt[un][:, PAIR:]) for un, x in zip(uns, ps)]
            y = [x[CHUNK:] + _dot(h["a_rb"][un], stack(w)) + av[un][CHUNK:]
                 for un, x, w in zip(uns, ps, u)]
            upd = [_dot_tn(jnp.concatenate([x.astype(BF16), h["v"][un]], axis=0), h["bk_end"][un])
                   for un, x in zip(uns, u)]
            yield
            for (dr, p, j), x in zip(uns, y):
                y_refs[dr][rows(g, dr, j), p * PAIR:(p + 1) * PAIR] = x
            state = [s * h["w_end"][un] + x * same_ref[...] for un, s, x in zip(uns, state, upd)]
            yield

    def level(h, lvl):
        sz = 1 << lvl
        m1, sel = {}, {}
        for un in units:
            dr = un[0]
            t = h["tinv"][un]
            if sz % BF16_ROWS == 0:
                sel[un] = [r0 for r0 in range(0, CHUNK, sz) if ((r0 & sz) != 0) != bool(dr)]
                t = jnp.concatenate([t[r0:r0 + sz] for r0 in sel[un]], axis=0)
            a2 = jnp.concatenate([h["a_ab"][un], h["a_ab"][un]], axis=0)
            m1[un] = _dot(t, a2 * lvls_ref[dr, lvl - 1])
        yield
        for un in units:
            t = h["tinv"][un]
            m2 = _dot(m1[un].astype(BF16), stack(t)).astype(BF16)
            if un in sel:
                blocks = [t[r0:r0 + sz] for r0 in range(0, CHUNK, sz)]
                for i, r0 in enumerate(sel[un]):
                    blocks[r0 // sz] = blocks[r0 // sz] - m2[i * sz:(i + 1) * sz]
                h["tinv"][un] = jnp.concatenate(blocks, axis=0)
            else:
                h["tinv"][un] = t - m2
        yield

    def advance(gens):
        for gen in gens:
            next(gen, None)

    def drain(gens):
        for gen in gens:
            for _ in gen:
                pass

    heads = [dict() for _ in range(n_groups)]
    drain([head(0, heads[0])])
    side = []
    for g in range(n_groups):
        h = heads[g]
        if g + 1 < n_groups:
            side.append(head(g + 1, heads[g + 1]))
        for lvl in range(1, n_levels):
            for _ in level(h, lvl):
                advance(side)
        drain(side)
        side = [tail(g, h)]
    drain(side)
    for i, s in enumerate(state):
        state_s[i] = s


def _wkv(r, k, v, z, mu, w0, a0, w2, k_k, k_a, bd, masks, *, bsz, seq, tb):
    n, d = r.shape
    nt = seq // tb
    n_pairs = d // PAIR
    fmap = lambda b, i: (b * nt + i, 0)
    bmap = lambda b, i: (b * nt + nt - 1 - i, 0)
    fblk, bblk = pl.BlockSpec((tb, d), fmap), pl.BlockSpec((tb, d), bmap)
    zf = pl.BlockSpec((tb, PAIR), fmap)
    zb = pl.BlockSpec((tb, PAIR), lambda b, i: (bmap(b, i)[0], 1))
    consts = [masks[key] for key in ("cum", "aa", "lvl0", "lvls", "eye", "heads", "same_head")]
    return pl.pallas_call(
        functools.partial(_wkv_kernel, tb=tb, n_pairs=n_pairs),
        grid=(bsz, nt),
        in_specs=[fblk, fblk, fblk, zf, bblk, bblk, bblk, zb,
                  _full(mu.shape), _full(w0.shape), _full(a0.shape), _full(w2.shape),
                  _full(k_k.shape), _full(k_a.shape), _full(bd.shape)] + [_full(m.shape) for m in consts],
        out_specs=[fblk, bblk],
        out_shape=[jax.ShapeDtypeStruct((n, d), F32)] * 2,
        scratch_shapes=[pltpu.VMEM((2 * n_pairs, PAIR, PAIR), F32), pltpu.VMEM((2, SUBLANES, PAIR), F32)],
        compiler_params=_params(2),
        name="wkv",
    )(r, k, v, z, r, k, v, z, mu, w0, a0, w2, k_k, k_a, bd, *consts)


def _shift_matrix(rows):
    shape = (2 * rows, rows + 2 * BF16_ROWS)
    r = lax.broadcasted_iota(jnp.int32, shape, 0)
    c = lax.broadcasted_iota(jnp.int32, shape, 1)
    target = jnp.where(r < rows, r + (BF16_ROWS - 1), r - rows + (BF16_ROWS + 1))
    return (c == target).astype(BF16)


def _conv3(x_ref, prev_ref, next_ref, shift, w_ref, b_ref, cols, first, last, r0):
    m = x_ref.shape[0]
    rows = shift.shape[0] // 2
    lo, hi = r0 - BF16_ROWS, r0 + rows + BF16_ROWS
    pieces = [x_ref[max(lo, 0):min(hi, m), cols]]
    if lo < 0:
        halo = prev_ref[:, cols]
        pieces.insert(0, jnp.where(first, jnp.zeros_like(halo), halo))
    if hi > m:
        halo = next_ref[:, cols]
        pieces.append(jnp.where(last, jnp.zeros_like(halo), halo))
    window = pieces[0] if len(pieces) == 1 else jnp.concatenate(pieces, axis=0)
    sh = _dot(shift, window)
    x = x_ref[r0:r0 + rows, cols].astype(F32)
    return (sh[:rows] * w_ref[0:1, cols] + x * w_ref[1:2, cols] + sh[rows:] * w_ref[2:3, cols]
            + b_ref[:, cols])


def _halo_specs(tm, width, rows_per_blk, n_rows):
    nb = tm // rows_per_blk
    last_blk = n_rows // rows_per_blk - 1
    prev = pl.BlockSpec((rows_per_blk, width), lambda i: (jnp.maximum(i * nb - 1, 0), 0))
    nxt = pl.BlockSpec((rows_per_blk, width), lambda i: (jnp.minimum((i + 1) * nb, last_blk), 0))
    return prev, nxt


def _mix_out_kernel(x_ref, yf_ref, yb_ref, r_ref, k_ref, v_ref, gd_ref, c_ref, cp_ref, cn_ref, bg_ref,
                    sgc_ref, sgr_ref, cw_ref, cb_ref, wa_ref, g2_ref, rk_ref, gnw_ref, gnb_ref,
                    bd_ref, wb_ref, wo_ref, g_ref, o_ref, *, tm, seq):
    i = pl.program_id(0)
    first = (i * tm) % seq == 0
    last = ((i + 1) * tm) % seq == 0
    rows = min(CONV_ROWS, tm)
    shift = _shift_matrix(rows)
    conv = jnp.concatenate([_conv3(c_ref, cp_ref, cn_ref, shift, cw_ref, cb_ref, slice(None), first,
                                   last, r0) for r0 in range(0, tm, rows)], axis=0)
    y_conv = _dot((bg_ref[...].astype(F32) * conv).astype(BF16), wa_ref[...])
    bd = bd_ref[...]
    y = yf_ref[...] + yb_ref[...]
    mean = _head_sum(y, bd) * (1.0 / HEAD)
    yc = y - mean
    var = _head_sum(yc * yc, bd) * (1.0 / HEAD)
    yn = yc * lax.rsqrt(var + GN_EPS) * gnw_ref[...] + gnb_ref[...]
    bonus = _head_sum(r_ref[...] * k_ref[...] * rk_ref[...], bd) * v_ref[...]
    gate = _dot(_sigmoid(gd_ref[...]).astype(BF16), g2_ref[...])
    y_rwkv = _dot(((yn + bonus) * gate).astype(BF16), wb_ref[...])
    merged = sgc_ref[...].astype(F32) * y_conv + sgr_ref[...].astype(F32) * y_rwkv
    m = _dot(merged.astype(BF16), wo_ref[...])
    o_ref[...] = x_ref[...] + _rms(m, g_ref[...])


def _mix_out(x, yf, yb, r, k, v, gd, cghc, bgate, sgc, sgr, conv_w, conv_b, w_a, g2, r_k, gn_w, gn_b,
             bd, w_b, w_out, g_post, *, tm, seq):
    n, d_model = x.shape
    d = r.shape[1]
    d_conv = cghc.shape[1]
    row = lambda w: pl.BlockSpec((tm, w), lambda i: (i, 0))
    cp, cn = _halo_specs(tm, d_conv, BF16_ROWS, n)
    return pl.pallas_call(
        functools.partial(_mix_out_kernel, tm=tm, seq=seq),
        grid=(n // tm,),
        in_specs=[row(d_model), row(d), row(d), row(d), row(d), row(d), row(gd.shape[1]),
                  row(d_conv), cp, cn, row(d_conv), row(d_model), row(d_model),
                  _full(conv_w.shape), _full(conv_b.shape), _full(w_a.shape), _full(g2.shape),
                  _full(r_k.shape), _full(gn_w.shape), _full(gn_b.shape), _full(bd.shape),
                  _full(w_b.shape), _full(w_out.shape), _full(g_post.shape)],
        out_specs=row(d_model),
        out_shape=jax.ShapeDtypeStruct((n, d_model), F32),
        compiler_params=_params(1),
        name="mix_out",
    )(x, yf, yb, r, k, v, gd, cghc, cghc, cghc, bgate, sgc, sgr, conv_w, conv_b, w_a, g2, r_k, gn_w,
      gn_b, bd, w_b, w_out, g_post)


def _ffn_kernel(x_ref, xp_ref, xn_ref, p_ref, gpre_ref, wu_ref, cw_ref, cb_ref, wd_ref, gf_ref, wp_ref,
                wg_ref, gp_ref, o_ref, act_s, *, tm, seq, d_ff, col_chunk):
    i = pl.program_id(0)
    first = (i * tm) % seq == 0
    last = ((i + 1) * tm) % seq == 0
    g = gpre_ref[...]
    u_prev = jnp.where(first, 0.0, _rms(xp_ref[...], g))
    u_next = jnp.where(last, 0.0, _rms(xn_ref[...], g))
    u = jnp.concatenate([u_prev, _rms(x_ref[...], g), u_next], axis=0).astype(BF16)
    m = tm + 2 * SUBLANES
    for c0 in range(0, d_ff, col_chunk):
        conv = []
        for off in (c0, d_ff + c0):
            cols = slice(off, off + col_chunk)
            h = _dot(u, wu_ref[:, cols])
            conv.append(pltpu.roll(h, 1, 0) * cw_ref[0:1, cols] + h * cw_ref[1:2, cols]
                        + pltpu.roll(h, m - 1, 0) * cw_ref[2:3, cols] + cb_ref[:, cols])
        hg, hv = conv
        gelu = 0.5 * hg * (1.0 + jnp.tanh(GELU_C * (hg + 0.044715 * (hg * hg * hg))))
        act_s[:, c0:c0 + col_chunk] = (gelu * hv)[SUBLANES:SUBLANES + tm].astype(BF16)
    f = _dot(act_s[...], wd_ref[...])
    x = x_ref[...] + _rms(f, gf_ref[...])
    gate = _sigmoid(_dot(x.astype(BF16), wg_ref[...]))
    pe = _dot(p_ref[...].astype(BF16), wp_ref[...])
    o_ref[...] = x + _rms(gate * pe, gp_ref[...])


def _ffn(x, p, g_pre, w_up, conv_w, conv_b, w_down, g_ffn, w_ple, w_gate, g_ple, *, tm, seq, col_chunk):
    n, d_model = x.shape
    d_ff = w_down.shape[0]
    row = lambda w: pl.BlockSpec((tm, w), lambda i: (i, 0))
    xp, xn = _halo_specs(tm, d_model, SUBLANES, n)
    const = lambda a: pl.BlockSpec(a.shape, lambda i: (0,) * a.ndim, pipeline_mode=pl.Buffered(1))
    return pl.pallas_call(
        functools.partial(_ffn_kernel, tm=tm, seq=seq, d_ff=d_ff, col_chunk=col_chunk),
        grid=(n // tm,),
        in_specs=[row(d_model), xp, xn, row(p.shape[1]), const(g_pre), const(w_up), const(conv_w),
                  const(conv_b), const(w_down), const(g_ffn), const(w_ple), const(w_gate), const(g_ple)],
        out_specs=row(d_model),
        out_shape=jax.ShapeDtypeStruct((n, d_model), F32),
        scratch_shapes=[pltpu.VMEM((tm, d_ff), BF16)],
        compiler_params=_params(1),
        name="ffn",
    )(x, x, x, p, g_pre, w_up, conv_w, conv_b, w_down, g_ffn, w_ple, w_gate, g_ple)


def _tiles(seq):
    tm = min(256, seq)
    tb = min(512, seq)
    return tm, tb


def _layer_weights(i, norm_mix_pre, norm_mix_post, norm_ffn_pre, norm_ffn_post, norm_ple_post, w_in,
                   conv_w, conv_b, w_branch_a, shift_mu, decay_w0, decay_w2, iclr_a0, iclr_a2, gate_g2,
                   k_k, k_a, r_k, gn_w, gn_b, w_branch_b, w_out, w_up, ffn_conv_w, ffn_conv_b, w_down,
                   w_ple, w_ple_gate):
    d_rwkv = k_k.shape[1]
    lora = decay_w2.shape[2]
    row = lambda a: a[i].reshape(1, -1)
    zeros = jnp.zeros((lora, d_rwkv), F32)
    lowrank = [jnp.concatenate([jnp.concatenate([decay_w2[i, d], zeros], axis=1),
                                jnp.concatenate([zeros, iclr_a2[i, d]], axis=1)], axis=0).astype(BF16)
               for d in range(2)]
    head_id = jnp.arange(d_rwkv) // HEAD
    return dict(
        g_mix_pre=row(norm_mix_pre), g_mix_post=row(norm_mix_post), g_ffn_pre=row(norm_ffn_pre),
        g_ffn_post=row(norm_ffn_post), g_ple_post=row(norm_ple_post),
        w_in=w_in[i].astype(BF16), conv_w=conv_w[i], conv_b=row(conv_b),
        w_a=w_branch_a[i].astype(BF16),
        mu=shift_mu[i][:, None, :], w0=decay_w0[i][:, None, :], a0=iclr_a0[i][:, None, :],
        lowrank=jnp.stack(lowrank), g2=gate_g2[i].astype(BF16), k_k=row(k_k), k_a=row(k_a),
        r_k=r_k[i].reshape(1, -1), gn_w=row(gn_w), gn_b=row(gn_b),
        bd=(head_id[:, None] == head_id[None, :]).astype(BF16),
        w_b=w_branch_b[i].astype(BF16), w_out=w_out[i].astype(BF16), w_up=w_up[i].astype(BF16),
        ffn_conv_w=ffn_conv_w[i], ffn_conv_b=row(ffn_conv_b), w_down=w_down[i].astype(BF16),
        w_ple=w_ple[i].astype(BF16), w_gate=w_ple_gate[i].astype(BF16))


def _layer(x, p, lw, masks, *, bsz, seq):
    tm, tb = _tiles(seq)
    d_conv = lw["conv_w"].shape[1]
    d_rwkv = lw["k_k"].shape[1]
    d_z = lw["mu"].shape[2]
    d_g = lw["g2"].shape[0]
    d_ff = lw["w_down"].shape[0]
    cghc, bgate, r, k, v, z, gd, sgc, sgr = _in_proj(
        x, lw["g_mix_pre"], lw["w_in"], tm=tm, d_conv=d_conv, d_rwkv=d_rwkv, d_z=2 * d_z, d_g=d_g)
    ys = _wkv(r, k, v, z, lw["mu"], lw["w0"], lw["a0"], lw["lowrank"], lw["k_k"], lw["k_a"], lw["bd"],
              masks, bsz=bsz, seq=seq, tb=tb)
    x = _mix_out(x, ys[0], ys[1], r, k, v, gd, cghc, bgate, sgc, sgr, lw["conv_w"], lw["conv_b"],
                 lw["w_a"], lw["g2"], lw["r_k"], lw["gn_w"], lw["gn_b"], lw["bd"], lw["w_b"],
                 lw["w_out"], lw["g_mix_post"], tm=tm, seq=seq)
    return _ffn(x, p, lw["g_ffn_pre"], lw["w_up"], lw["ffn_conv_w"], lw["ffn_conv_b"], lw["w_down"],
                lw["g_ffn_post"], lw["w_ple"], lw["w_gate"], lw["g_ple_post"], tm=tm, seq=seq,
                col_chunk=math.gcd(d_ff, 256))


def kernel(x_prompt, x_sample, p_prompt, p_sample, norm_mix_pre, norm_mix_post, norm_ffn_pre, norm_ffn_post, norm_ple_post, w_in, conv_w, conv_b, w_branch_a, shift_mu, decay_w0, decay_w2, iclr_a0, iclr_a2, gate_g2, k_k, k_a, r_k, gn_w, gn_b, w_branch_b, w_out, w_up, ffn_conv_w, ffn_conv_b, w_down, w_ple, w_ple_gate):
    weights = (norm_mix_pre, norm_mix_post, norm_ffn_pre, norm_ffn_post, norm_ple_post, w_in, conv_w,
               conv_b, w_branch_a, shift_mu, decay_w0, decay_w2, iclr_a0, iclr_a2, gate_g2, k_k, k_a,
               r_k, gn_w, gn_b, w_branch_b, w_out, w_up, ffn_conv_w, ffn_conv_b, w_down, w_ple,
               w_ple_gate)
    layers = [_layer_weights(i, *weights) for i in range(w_in.shape[0])]
    masks = _wkv_masks()
    outs = []
    for x, p in ((x_prompt, p_prompt), (x_sample, p_sample)):
        bsz, seq, d_model = x.shape
        y = x.reshape(bsz * seq, d_model)
        for i, lw in enumerate(layers):
            y = _layer(y, p[i].reshape(bsz * seq, -1), lw, masks, bsz=bsz, seq=seq)
        outs.append(y.reshape(bsz, seq, d_model))
    return tuple(outs)
```

```python
import functools
import math

import jax
import jax.numpy as jnp
from jax import lax
from jax.experimental import pallas as pl
from jax.experimental.pallas import tpu as pltpu

F32 = jnp.float32
BF16 = jnp.bfloat16

LANES = 128
SUBLANES = 8
BF16_ROWS = 16
VMEM_LIMIT_BYTES = 56 * 1024 * 1024

HEAD = 64
PAIR = 2 * HEAD
CHUNK = 64
CHUNKS_PER_STEP = 2
CONV_ROWS = 128
NORM_EPS = 1e-6
GN_EPS = HEAD * 1e-5
DECAY_SCALE = math.exp(-0.5)
GELU_C = math.sqrt(2.0 / math.pi)


def _sigmoid(x):
    return 1.0 / (1.0 + jnp.exp(-x))


def _rms(x, g):
    return x * lax.rsqrt(jnp.mean(x * x, axis=-1, keepdims=True) + NORM_EPS) * g


def _dot(a, b):
    return jnp.dot(a, b, preferred_element_type=F32)


def _dot_nt(a, b):
    return lax.dot_general(a, b, (((1,), (1,)), ((), ())), preferred_element_type=F32)


def _dot_tn(a, b):
    return lax.dot_general(a, b, (((0,), (0,)), ((), ())), preferred_element_type=F32)


def _split2(x):
    hi = x.astype(BF16)
    lo = (x - hi.astype(F32)).astype(BF16)
    return hi, lo


def _head_sum(x, bd):
    return _dot(x.astype(BF16), bd)


def _params(n_axes):
    return pltpu.CompilerParams(dimension_semantics=("arbitrary",) * n_axes,
                                vmem_limit_bytes=VMEM_LIMIT_BYTES)


def _full(shape):
    nd = len(shape)
    return pl.BlockSpec(shape, lambda *_: (0,) * nd)


def _in_proj_kernel(x_ref, g_ref, w_ref, cghc_ref, bg_ref, r_ref, k_ref, v_ref, z_ref, gd_ref,
                    sgc_ref, sgr_ref, *, d_conv, d_rwkv, d_z, d_g, d_model):
    u = _rms(x_ref[...], g_ref[...]).astype(BF16)
    o = 0
    hbc = _dot(u, w_ref[:, o:o + 3 * d_conv])
    cghc_ref[...] = (hbc[:, 2 * d_conv:] * hbc[:, :d_conv]).astype(BF16)
    bg_ref[...] = hbc[:, d_conv:2 * d_conv].astype(BF16)
    o += 3 * d_conv
    for ref in (r_ref, k_ref, v_ref):
        ref[...] = _dot(u, w_ref[:, o:o + d_rwkv])
        o += d_rwkv
    z_ref[...] = _dot(u, w_ref[:, o:o + d_z])
    o += d_z
    gd_ref[...] = _dot(u, w_ref[:, o:o + d_g])
    o += d_g
    for ref in (sgc_ref, sgr_ref):
        ref[...] = _sigmoid(_dot(u, w_ref[:, o:o + d_model])).astype(BF16)
        o += d_model


def _in_proj(x, g, w_in, *, tm, d_conv, d_rwkv, d_z, d_g):
    n, d_model = x.shape
    cols = w_in.shape[1]
    row = lambda w: pl.BlockSpec((tm, w), lambda i: (i, 0))
    widths = (d_conv, d_conv, d_rwkv, d_rwkv, d_rwkv, d_z, d_g, d_model, d_model)
    dtypes = (BF16, BF16, F32, F32, F32, F32, F32, BF16, BF16)
    return pl.pallas_call(
        functools.partial(_in_proj_kernel, d_conv=d_conv, d_rwkv=d_rwkv, d_z=d_z, d_g=d_g,
                          d_model=d_model),
        grid=(n // tm,),
        in_specs=[row(d_model), _full((1, d_model)), _full((d_model, cols))],
        out_specs=[row(w) for w in widths],
        out_shape=[jax.ShapeDtypeStruct((n, w), dt) for w, dt in zip(widths, dtypes)],
        compiler_params=_params(1),
        name="in_proj",
    )(x, g, w_in)


def _wkv_masks():
    ti = lax.broadcasted_iota(jnp.int32, (CHUNK, PAIR), 0)
    lane = lax.broadcasted_iota(jnp.int32, (CHUNK, PAIR), 1)
    si = lane & (HEAD - 1)
    li = lax.broadcasted_iota(jnp.int32, (CHUNK, CHUNK), 0)
    lj = lax.broadcasted_iota(jnp.int32, (CHUNK, CHUNK), 1)
    head_lo = (lane < HEAD).astype(BF16)
    cum, aa, lvl0, lvls = [], [], [], []
    for reverse in (False, True):
        strict = ((si > ti) if reverse else (si < ti)).astype(BF16)
        incl = ((si >= ti) if reverse else (si <= ti)).astype(BF16)
        cum.append(((lj >= li) if reverse else (lj <= li)).astype(BF16))
        aa.append(jnp.concatenate([jnp.concatenate([strict, strict], axis=1),
                                   jnp.concatenate([incl, incl], axis=1)], axis=0))
        levels = []
        sz = 1
        while sz < CHUNK:
            same_blk = (ti & -(2 * sz)) == (si & -(2 * sz))
            t_hi, s_hi = (ti & sz) != 0, (si & sz) != 0
            levels.append((same_blk & ((~t_hi & s_hi) if reverse else (t_hi & ~s_hi))).astype(BF16))
            sz *= 2
        lvl0.append(levels[0])
        lvls.append(jnp.stack([jnp.concatenate([m * head_lo, m * (1 - head_lo)], axis=0)
                               for m in levels[1:]]))
    bi = lax.broadcasted_iota(jnp.int32, (PAIR, PAIR), 0)
    bj = lax.broadcasted_iota(jnp.int32, (PAIR, PAIR), 1)
    return dict(cum=jnp.stack(cum), aa=jnp.stack(aa), lvl0=jnp.stack(lvl0), lvls=jnp.stack(lvls),
                eye=(si == ti).astype(BF16), heads=jnp.stack([head_lo, 1 - head_lo]),
                same_head=((bi < HEAD) == (bj < HEAD)).astype(F32))


def _wkv_features(z_ref, mu, w2, zc_s, *, reverse, tb):
    z = z_ref[...]
    rows = lax.broadcasted_iota(jnp.int32, z.shape, 0)
    carry = jnp.broadcast_to(zc_s[0:1, :], z.shape)
    if reverse:
        zs = jnp.where(rows == tb - 1, carry, pltpu.roll(z, tb - 1, 0))
        zc_s[...] = jnp.broadcast_to(z[0:1, :], zc_s.shape)
    else:
        zs = jnp.where(rows == 0, carry, pltpu.roll(z, 1, 0))
        zc_s[...] = jnp.broadcast_to(z[tb - 1:tb, :], zc_s.shape)
    zm = z + mu * (zs - z)
    lanes = lax.broadcasted_iota(jnp.int32, z.shape, 1)
    feat = jnp.where(lanes < HEAD, jnp.tanh(zm), zm).astype(BF16)
    return _dot(feat, w2)


def _wkv_prep(lo, k, w0, a0, k_k, k_a, bd):
    d = k.shape[1]
    lw = -DECAY_SCALE * _sigmoid(w0 + lo[:, :d])
    a = _sigmoid(a0 + lo[:, d:])
    kkr = k * k_k
    kkn = kkr * lax.rsqrt(jnp.maximum(_head_sum(kkr * kkr, bd), 1e-24))
    return lw, kkn, kkn * a, k * (1.0 + (a - 1.0) * k_a)


def _wkv_kernel(rf_ref, kf_ref, vf_ref, zf_ref, rb_ref, kb_ref, vb_ref, zb_ref, mu_ref, w0_ref, a0_ref,
                w2_ref, kk_ref, ka_ref, rk_ref, bd_ref, cum_ref, aa_ref, lvl0_ref, lvls_ref, eye_ref,
                heads_ref, same_ref, yf_ref, yb_ref, bonus_ref, state_s, zc_s, *, tb, n_pairs):
    nch = tb // CHUNK
    r_refs, k_refs, v_refs, z_refs, y_refs = ((rf_ref, rb_ref), (kf_ref, kb_ref), (vf_ref, vb_ref),
                                              (zf_ref, zb_ref), (yf_ref, yb_ref))

    @pl.when(pl.program_id(1) == 0)
    def _():
        state_s[...] = jnp.zeros_like(state_s)
        zc_s[...] = jnp.zeros_like(zc_s)

    lo = [_wkv_features(z_refs[dr], mu_ref[dr], w2_ref[dr], zc_s.at[dr], reverse=bool(dr), tb=tb)
          for dr in range(2)]
    edge = (CHUNK - 1, 0)

    def stack(x):
        xb = x.astype(BF16)
        return jnp.concatenate([xb * heads_ref[0], xb * heads_ref[1]], axis=0)

    group = math.gcd(nch, CHUNKS_PER_STEP)
    span = group * CHUNK
    n_groups = nch // group
    units = [(dr, p, j) for j in range(group) for dr in range(2) for p in range(n_pairs)]
    halves = (units[:len(units) // 2], units[len(units) // 2:])
    n_levels = lvls_ref.shape[1] + 1
    state = [state_s[i] for i in range(2 * n_pairs)]

    def rows(g, dr, j):
        base = g * span if dr == 0 else tb - (g + 1) * span
        off = base + (j if dr == 0 else group - 1 - j) * CHUNK
        return slice(off, off + CHUNK)

    def head(g, out):
        prep = {}
        for dr in range(2):
            blk = slice(rows(g, dr, 0 if dr == 0 else group - 1).start,
                        rows(g, dr, group - 1 if dr == 0 else 0).stop)
            prep[dr] = _wkv_prep(lo[dr][blk], k_refs[dr][blk, :], w0_ref[dr], a0_ref[dr], kk_ref[...],
                                 ka_ref[...], bd_ref[...])
            if dr == 0:
                rk_sum = _head_sum(rf_ref[blk, :] * kf_ref[blk, :] * rk_ref[...], bd_ref[...])
                bonus_ref[blk, :] = (rk_sum * vf_ref[blk, :]).astype(BF16)
            yield

        def take(which, dr, p, j):
            off = (j if dr == 0 else group - 1 - j) * CHUNK
            return prep[dr][which][off:off + CHUNK, p * PAIR:(p + 1) * PAIR]

        def load(refs, dr, p, j):
            return refs[dr][rows(g, dr, j), p * PAIR:(p + 1) * PAIR]

        for key in ("cw", "rw", "kkw", "v", "a_ab", "a_rb", "a_kv", "tinv", "w_end", "bk_end"):
            out[key] = {}
        for part in halves:
            for un in part:
                dr, p, j = un
                h1, h2 = _split2(take(0, dr, p, j))
                cs = _dot(cum_ref[dr], jnp.concatenate([h1, h2], axis=1))
                out["cw"][un] = cs[:, :PAIR] + cs[:, PAIR:]
            yield
        for part in halves:
            for un in part:
                dr, p, j = un
                cw = out["cw"].pop(un)
                e_pos, e_neg = jnp.exp(cw), jnp.exp(-cw)
                rw = (load(r_refs, dr, p, j) * e_pos).astype(BF16)
                kkw = (take(1, dr, p, j) * jnp.exp(cw - take(0, dr, p, j))).astype(BF16)
                binv = take(2, dr, p, j) * e_neg
                kinv = take(3, dr, p, j) * e_neg
                w_end = e_pos[edge[dr]:edge[dr] + 1, :]
                out["rw"][un], out["kkw"][un], out["w_end"][un] = rw, kkw, w_end
                out["v"][un] = load(v_refs, dr, p, j).astype(BF16)
                out["bk_end"][un] = jnp.concatenate([binv * w_end, kinv * w_end], axis=0).astype(BF16)
                aa = (_dot_nt(jnp.concatenate([kkw, rw], axis=0),
                              jnp.concatenate([stack(binv), stack(kinv)], axis=0)).astype(BF16)
                      * aa_ref[dr])
                a_ab = aa[:CHUNK, :PAIR]
                out["a_ab"][un] = a_ab
                out["a_rb"][un] = aa[CHUNK:, :PAIR]
                out["a_kv"][un] = aa[:, PAIR:]
                out["tinv"][un] = eye_ref[...] - a_ab * lvl0_ref[dr]
            yield

    def tail(g, h):
        nonlocal state
        av, tt = {}, {}
        for part in halves:
            for un in part:
                av[un] = _dot(h["a_kv"][un], stack(h["v"][un]))
            yield
        for part in halves:
            for un in part:
                tt[un] = _dot(h["tinv"][un],
                              jnp.concatenate([stack(h["kkw"][un]), stack(av[un][:CHUNK])], axis=1))
            yield
        for jj in range(group):
            uns = [un for un in units if un[2] == jj]
            ps = [_dot_nt(jnp.concatenate([tt[un][:, :PAIR].astype(BF16), h["rw"][un]], axis=0),
                          s.astype(BF16)) for un, s in zip(uns, state)]
            yield
            u = [-(x[:CHUNK] + tt[un][:, PAIR:]) for un, x in zip(uns, ps)]
            y = [x[CHUNK:] + _dot(h["a_rb"][un], stack(w)) + av[un][CHUNK:]
                 for un, x, w in zip(uns, ps, u)]
            upd = [_dot_tn(jnp.concatenate([x.astype(BF16), h["v"][un]], axis=0), h["bk_end"][un])
                   for un, x in zip(uns, u)]
            yield
            for (dr, p, j), x in zip(uns, y):
                y_refs[dr][rows(g, dr, j), p * PAIR:(p + 1) * PAIR] = x.astype(BF16)
            state = [s * h["w_end"][un] + x * same_ref[...] for un, s, x in zip(uns, state, upd)]
            yield

    def level(h, lvl):
        sz = 1 << lvl
        m1, sel = {}, {}
        for un in units:
            dr = un[0]
            t = h["tinv"][un]
            if sz % BF16_ROWS == 0:
                sel[un] = [r0 for r0 in range(0, CHUNK, sz) if ((r0 & sz) != 0) != bool(dr)]
                t = jnp.concatenate([t[r0:r0 + sz] for r0 in sel[un]], axis=0)
            a2 = jnp.concatenate([h["a_ab"][un], h["a_ab"][un]], axis=0)
            m1[un] = _dot(t, a2 * lvls_ref[dr, lvl - 1])
        yield
        for un in units:
            t = h["tinv"][un]
            m2 = _dot(m1[un].astype(BF16), stack(t)).astype(BF16)
            if un in sel:
                blocks = [t[r0:r0 + sz] for r0 in range(0, CHUNK, sz)]
                for i, r0 in enumerate(sel[un]):
                    blocks[r0 // sz] = blocks[r0 // sz] - m2[i * sz:(i + 1) * sz]
                h["tinv"][un] = jnp.concatenate(blocks, axis=0)
            else:
                h["tinv"][un] = t - m2
        yield

    def advance(gens):
        for gen in gens:
            next(gen, None)

    def drain(gens):
        for gen in gens:
            for _ in gen:
                pass

    heads = [dict() for _ in range(n_groups)]
    drain([head(0, heads[0])])
    side = []
    for g in range(n_groups):
        h = heads[g]
        if g + 1 < n_groups:
            side.append(head(g + 1, heads[g + 1]))
        for lvl in range(1, n_levels):
            for _ in level(h, lvl):
                advance(side)
        drain(side)
        side = [tail(g, h)]
    drain(side)
    for i, s in enumerate(state):
        state_s[i] = s


def _wkv(r, k, v, z, mu, w0, a0, w2, k_k, k_a, r_k, bd, masks, *, bsz, seq, tb):
    n, d = r.shape
    nt = seq // tb
    n_pairs = d // PAIR
    fmap = lambda b, i: (b * nt + i, 0)
    bmap = lambda b, i: (b * nt + nt - 1 - i, 0)
    fblk, bblk = pl.BlockSpec((tb, d), fmap), pl.BlockSpec((tb, d), bmap)
    zf = pl.BlockSpec((tb, PAIR), fmap)
    zb = pl.BlockSpec((tb, PAIR), lambda b, i: (bmap(b, i)[0], 1))
    consts = [masks[key] for key in ("cum", "aa", "lvl0", "lvls", "eye", "heads", "same_head")]
    return pl.pallas_call(
        functools.partial(_wkv_kernel, tb=tb, n_pairs=n_pairs),
        grid=(bsz, nt),
        in_specs=[fblk, fblk, fblk, zf, bblk, bblk, bblk, zb,
                  _full(mu.shape), _full(w0.shape), _full(a0.shape), _full(w2.shape),
                  _full(k_k.shape), _full(k_a.shape), _full(r_k.shape), _full(bd.shape)]
        + [_full(m.shape) for m in consts],
        out_specs=[fblk, bblk, fblk],
        out_shape=[jax.ShapeDtypeStruct((n, d), BF16)] * 3,
        scratch_shapes=[pltpu.VMEM((2 * n_pairs, PAIR, PAIR), F32), pltpu.VMEM((2, SUBLANES, PAIR), F32)],
        compiler_params=_params(2),
        name="wkv",
    )(r, k, v, z, r, k, v, z, mu, w0, a0, w2, k_k, k_a, r_k, bd, *consts)


def _shift_matrix(rows):
    shape = (2 * rows, rows + 2 * BF16_ROWS)
    r = lax.broadcasted_iota(jnp.int32, shape, 0)
    c = lax.broadcasted_iota(jnp.int32, shape, 1)
    target = jnp.where(r < rows, r + (BF16_ROWS - 1), r - rows + (BF16_ROWS + 1))
    return (c == target).astype(BF16)


def _conv3(x_ref, prev_ref, next_ref, shift, w_ref, b_ref, cols, first, last, r0):
    m = x_ref.shape[0]
    rows = shift.shape[0] // 2
    lo, hi = r0 - BF16_ROWS, r0 + rows + BF16_ROWS
    pieces = [x_ref[max(lo, 0):min(hi, m), cols]]
    if lo < 0:
        halo = prev_ref[:, cols]
        pieces.insert(0, jnp.where(first, jnp.zeros_like(halo), halo))
    if hi > m:
        halo = next_ref[:, cols]
        pieces.append(jnp.where(last, jnp.zeros_like(halo), halo))
    window = pieces[0] if len(pieces) == 1 else jnp.concatenate(pieces, axis=0)
    sh = _dot(shift, window)
    x = x_ref[r0:r0 + rows, cols].astype(F32)
    return (sh[:rows] * w_ref[0:1, cols] + x * w_ref[1:2, cols] + sh[rows:] * w_ref[2:3, cols]
            + b_ref[:, cols])


def _halo_specs(tm, width, rows_per_blk, n_rows):
    nb = tm // rows_per_blk
    last_blk = n_rows // rows_per_blk - 1
    prev = pl.BlockSpec((rows_per_blk, width), lambda i: (jnp.maximum(i * nb - 1, 0), 0))
    nxt = pl.BlockSpec((rows_per_blk, width), lambda i: (jnp.minimum((i + 1) * nb, last_blk), 0))
    return prev, nxt


def _mix_out_kernel(x_ref, yf_ref, yb_ref, bonus_ref, gd_ref, c_ref, cp_ref, cn_ref, bg_ref,
                    sgc_ref, sgr_ref, cw_ref, cb_ref, wa_ref, g2_ref, gnw_ref, gnb_ref,
                    bd_ref, wb_ref, wo_ref, g_ref, o_ref, *, tm, seq):
    i = pl.program_id(0)
    first = (i * tm) % seq == 0
    last = ((i + 1) * tm) % seq == 0
    rows = min(CONV_ROWS, tm)
    shift = _shift_matrix(rows)
    conv = jnp.concatenate([_conv3(c_ref, cp_ref, cn_ref, shift, cw_ref, cb_ref, slice(None), first,
                                   last, r0) for r0 in range(0, tm, rows)], axis=0)
    y_conv = _dot((bg_ref[...].astype(F32) * conv).astype(BF16), wa_ref[...])
    bd = bd_ref[...]
    y = yf_ref[...].astype(F32) + yb_ref[...].astype(F32)
    mean = _head_sum(y, bd) * (1.0 / HEAD)
    yc = y - mean
    var = _head_sum(yc * yc, bd) * (1.0 / HEAD)
    yn = yc * lax.rsqrt(var + GN_EPS) * gnw_ref[...] + gnb_ref[...]
    gate = _dot(_sigmoid(gd_ref[...]).astype(BF16), g2_ref[...])
    y_rwkv = _dot(((yn + bonus_ref[...].astype(F32)) * gate).astype(BF16), wb_ref[...])
    merged = sgc_ref[...].astype(F32) * y_conv + sgr_ref[...].astype(F32) * y_rwkv
    m = _dot(merged.astype(BF16), wo_ref[...])
    o_ref[...] = x_ref[...] + _rms(m, g_ref[...])


def _mix_out(x, yf, yb, bonus, gd, cghc, bgate, sgc, sgr, conv_w, conv_b, w_a, g2, gn_w, gn_b,
             bd, w_b, w_out, g_post, *, tm, seq):
    n, d_model = x.shape
    d = yf.shape[1]
    d_conv = cghc.shape[1]
    row = lambda w: pl.BlockSpec((tm, w), lambda i: (i, 0))
    cp, cn = _halo_specs(tm, d_conv, BF16_ROWS, n)
    return pl.pallas_call(
        functools.partial(_mix_out_kernel, tm=tm, seq=seq),
        grid=(n // tm,),
        in_specs=[row(d_model), row(d), row(d), row(d), row(gd.shape[1]),
                  row(d_conv), cp, cn, row(d_conv), row(d_model), row(d_model),
                  _full(conv_w.shape), _full(conv_b.shape), _full(w_a.shape), _full(g2.shape),
                  _full(gn_w.shape), _full(gn_b.shape), _full(bd.shape),
                  _full(w_b.shape), _full(w_out.shape), _full(g_post.shape)],
        out_specs=row(d_model),
        out_shape=jax.ShapeDtypeStruct((n, d_model), F32),
        compiler_params=_params(1),
        name="mix_out",
    )(x, yf, yb, bonus, gd, cghc, cghc, cghc, bgate, sgc, sgr, conv_w, conv_b, w_a, g2, gn_w,
      gn_b, bd, w_b, w_out, g_post)


def _ffn_kernel(x_ref, xp_ref, xn_ref, p_ref, gpre_ref, wu_ref, cw_ref, cb_ref, wd_ref, gf_ref, wp_ref,
                wg_ref, gp_ref, o_ref, act_s, *, tm, seq, d_ff, col_chunk):
    i = pl.program_id(0)
    first = (i * tm) % seq == 0
    last = ((i + 1) * tm) % seq == 0
    g = gpre_ref[...]
    u_prev = jnp.where(first, 0.0, _rms(xp_ref[...], g))
    u_next = jnp.where(last, 0.0, _rms(xn_ref[...], g))
    u = jnp.concatenate([u_prev, _rms(x_ref[...], g), u_next], axis=0).astype(BF16)
    m = tm + 2 * SUBLANES
    for c0 in range(0, d_ff, col_chunk):
        conv = []
        for off in (c0, d_ff + c0):
            cols = slice(off, off + col_chunk)
            h = _dot(u, wu_ref[:, cols])
            conv.append(pltpu.roll(h, 1, 0) * cw_ref[0:1, cols] + h * cw_ref[1:2, cols]
                        + pltpu.roll(h, m - 1, 0) * cw_ref[2:3, cols] + cb_ref[:, cols])
        hg, hv = conv
        gelu = 0.5 * hg * (1.0 + jnp.tanh(GELU_C * (hg + 0.044715 * (hg * hg * hg))))
        act_s[:, c0:c0 + col_chunk] = (gelu * hv)[SUBLANES:SUBLANES + tm].astype(BF16)
    f = _dot(act_s[...], wd_ref[...])
    x = x_ref[...] + _rms(f, gf_ref[...])
    gate = _sigmoid(_dot(x.astype(BF16), wg_ref[...]))
    pe = _dot(p_ref[...].astype(BF16), wp_ref[...])
    o_ref[...] = x + _rms(gate * pe, gp_ref[...])


def _ffn(x, p, g_pre, w_up, conv_w, conv_b, w_down, g_ffn, w_ple, w_gate, g_ple, *, tm, seq, col_chunk):
    n, d_model = x.shape
    d_ff = w_down.shape[0]
    row = lambda w: pl.BlockSpec((tm, w), lambda i: (i, 0))
    xp, xn = _halo_specs(tm, d_model, SUBLANES, n)
    const = lambda a: pl.BlockSpec(a.shape, lambda i: (0,) * a.ndim, pipeline_mode=pl.Buffered(1))
    return pl.pallas_call(
        functools.partial(_ffn_kernel, tm=tm, seq=seq, d_ff=d_ff, col_chunk=col_chunk),
        grid=(n // tm,),
        in_specs=[row(d_model), xp, xn, row(p.shape[1]), const(g_pre), const(w_up), const(conv_w),
                  const(conv_b), const(w_down), const(g_ffn), const(w_ple), const(w_gate), const(g_ple)],
        out_specs=row(d_model),
        out_shape=jax.ShapeDtypeStruct((n, d_model), F32),
        scratch_shapes=[pltpu.VMEM((tm, d_ff), BF16)],
        compiler_params=_params(1),
        name="ffn",
    )(x, x, x, p, g_pre, w_up, conv_w, conv_b, w_down, g_ffn, w_ple, w_gate, g_ple)


def _tiles(seq):
    tm = min(256, seq)
    tb = min(512, seq)
    return tm, tb


def _layer_weights(i, norm_mix_pre, norm_mix_post, norm_ffn_pre, norm_ffn_post, norm_ple_post, w_in,
                   conv_w, conv_b, w_branch_a, shift_mu, decay_w0, decay_w2, iclr_a0, iclr_a2, gate_g2,
                   k_k, k_a, r_k, gn_w, gn_b, w_branch_b, w_out, w_up, ffn_conv_w, ffn_conv_b, w_down,
                   w_ple, w_ple_gate):
    d_rwkv = k_k.shape[1]
    lora = decay_w2.shape[2]
    row = lambda a: a[i].reshape(1, -1)
    zeros = jnp.zeros((lora, d_rwkv), F32)
    lowrank = [jnp.concatenate([jnp.concatenate([decay_w2[i, d], zeros], axis=1),
                                jnp.concatenate([zeros, iclr_a2[i, d]], axis=1)], axis=0).astype(BF16)
               for d in range(2)]
    head_id = jnp.arange(d_rwkv) // HEAD
    return dict(
        g_mix_pre=row(norm_mix_pre), g_mix_post=row(norm_mix_post), g_ffn_pre=row(norm_ffn_pre),
        g_ffn_post=row(norm_ffn_post), g_ple_post=row(norm_ple_post),
        w_in=w_in[i].astype(BF16), conv_w=conv_w[i], conv_b=row(conv_b),
        w_a=w_branch_a[i].astype(BF16),
        mu=shift_mu[i][:, None, :], w0=decay_w0[i][:, None, :], a0=iclr_a0[i][:, None, :],
        lowrank=jnp.stack(lowrank), g2=gate_g2[i].astype(BF16), k_k=row(k_k), k_a=row(k_a),
        r_k=r_k[i].reshape(1, -1), gn_w=row(gn_w), gn_b=row(gn_b),
        bd=(head_id[:, None] == head_id[None, :]).astype(BF16),
        w_b=w_branch_b[i].astype(BF16), w_out=w_out[i].astype(BF16), w_up=w_up[i].astype(BF16),
        ffn_conv_w=ffn_conv_w[i], ffn_conv_b=row(ffn_conv_b), w_down=w_down[i].astype(BF16),
        w_ple=w_ple[i].astype(BF16), w_gate=w_ple_gate[i].astype(BF16))


def _layer(x, p, lw, masks, *, bsz, seq):
    tm, tb = _tiles(seq)
    d_conv = lw["conv_w"].shape[1]
    d_rwkv = lw["k_k"].shape[1]
    d_z = lw["mu"].shape[2]
    d_g = lw["g2"].shape[0]
    d_ff = lw["w_down"].shape[0]
    cghc, bgate, r, k, v, z, gd, sgc, sgr = _in_proj(
        x, lw["g_mix_pre"], lw["w_in"], tm=tm, d_conv=d_conv, d_rwkv=d_rwkv, d_z=2 * d_z, d_g=d_g)
    yf, yb, bonus = _wkv(r, k, v, z, lw["mu"], lw["w0"], lw["a0"], lw["lowrank"], lw["k_k"], lw["k_a"],
                         lw["r_k"], lw["bd"], masks, bsz=bsz, seq=seq, tb=tb)
    x = _mix_out(x, yf, yb, bonus, gd, cghc, bgate, sgc, sgr, lw["conv_w"], lw["conv_b"],
                 lw["w_a"], lw["g2"], lw["gn_w"], lw["gn_b"], lw["bd"], lw["w_b"],
                 lw["w_out"], lw["g_mix_post"], tm=tm, seq=seq)
    return _ffn(x, p, lw["g_ffn_pre"], lw["w_up"], lw["ffn_conv_w"], lw["ffn_conv_b"], lw["w_down"],
                lw["g_ffn_post"], lw["w_ple"], lw["w_gate"], lw["g_ple_post"], tm=tm, seq=seq,
                col_chunk=math.gcd(d_ff, 256))


def kernel(x_prompt, x_sample, p_prompt, p_sample, norm_mix_pre, norm_mix_post, norm_ffn_pre, norm_ffn_post, norm_ple_post, w_in, conv_w, conv_b, w_branch_a, shift_mu, decay_w0, decay_w2, iclr_a0, iclr_a2, gate_g2, k_k, k_a, r_k, gn_w, gn_b, w_branch_b, w_out, w_up, ffn_conv_w, ffn_conv_b, w_down, w_ple, w_ple_gate):
    weights = (norm_mix_pre, norm_mix_post, norm_ffn_pre, norm_ffn_post, norm_ple_post, w_in, conv_w,
               conv_b, w_branch_a, shift_mu, decay_w0, decay_w2, iclr_a0, iclr_a2, gate_g2, k_k, k_a,
               r_k, gn_w, gn_b, w_branch_b, w_out, w_up, ffn_conv_w, ffn_conv_b, w_down, w_ple,
               w_ple_gate)
    layers = [_layer_weights(i, *weights) for i in range(w_in.shape[0])]
    masks = _wkv_masks()
    outs = []
    for x, p in ((x_prompt, p_prompt), (x_sample, p_sample)):
        bsz, seq, d_model = x.shape
        y = x.reshape(bsz * seq, d_model)
        for i, lw in enumerate(layers):
            y = _layer(y, p[i].reshape(bsz * seq, -1), lw, masks, bsz=bsz, seq=seq)
        outs.append(y.reshape(bsz, seq, d_model))
    return tuple(outs)
```

```python
import functools
import math

import jax
import jax.numpy as jnp
from jax import lax
from jax.experimental import pallas as pl
from jax.experimental.pallas import tpu as pltpu

F32 = jnp.float32
BF16 = jnp.bfloat16

LANES = 128
SUBLANES = 8
BF16_ROWS = 16
VMEM_LIMIT_BYTES = 56 * 1024 * 1024

HEAD = 64
PAIR = 2 * HEAD
CHUNK = 64
CHUNKS_PER_STEP = 2
CONV_ROWS = 128
NORM_EPS = 1e-6
GN_EPS = HEAD * 1e-5
DECAY_SCALE = math.exp(-0.5)
GELU_C = math.sqrt(2.0 / math.pi)


def _sigmoid(x):
    return 1.0 / (1.0 + jnp.exp(-x))


def _rms(x, g):
    return x * lax.rsqrt(jnp.mean(x * x, axis=-1, keepdims=True) + NORM_EPS) * g


def _dot(a, b):
    return jnp.dot(a, b, preferred_element_type=F32)


def _dot_nt(a, b):
    return lax.dot_general(a, b, (((1,), (1,)), ((), ())), preferred_element_type=F32)


def _dot_tn(a, b):
    return lax.dot_general(a, b, (((0,), (0,)), ((), ())), preferred_element_type=F32)


def _split2(x):
    hi = x.astype(BF16)
    lo = (x - hi.astype(F32)).astype(BF16)
    return hi, lo


def _head_sum(x, bd):
    return _dot(x.astype(BF16), bd)


def _params(n_axes):
    return pltpu.CompilerParams(dimension_semantics=("arbitrary",) * n_axes,
                                vmem_limit_bytes=VMEM_LIMIT_BYTES)


def _full(shape):
    nd = len(shape)
    return pl.BlockSpec(shape, lambda *_: (0,) * nd)


def _in_proj_kernel(x_ref, g_ref, w_ref, cghc_ref, bg_ref, r_ref, k_ref, v_ref, z_ref, gd_ref,
                    sgc_ref, sgr_ref, *, d_conv, d_rwkv, d_z, d_g, d_model):
    u = _rms(x_ref[...], g_ref[...]).astype(BF16)
    o = 0
    hbc = _dot(u, w_ref[:, o:o + 3 * d_conv])
    cghc_ref[...] = (hbc[:, 2 * d_conv:] * hbc[:, :d_conv]).astype(BF16)
    bg_ref[...] = hbc[:, d_conv:2 * d_conv].astype(BF16)
    o += 3 * d_conv
    for ref in (r_ref, k_ref, v_ref):
        ref[...] = _dot(u, w_ref[:, o:o + d_rwkv])
        o += d_rwkv
    z_ref[...] = _dot(u, w_ref[:, o:o + d_z])
    o += d_z
    gd_ref[...] = _dot(u, w_ref[:, o:o + d_g])
    o += d_g
    for ref in (sgc_ref, sgr_ref):
        ref[...] = _sigmoid(_dot(u, w_ref[:, o:o + d_model])).astype(BF16)
        o += d_model


def _in_proj(x, g, w_in, *, tm, d_conv, d_rwkv, d_z, d_g):
    n, d_model = x.shape
    cols = w_in.shape[1]
    row = lambda w: pl.BlockSpec((tm, w), lambda i: (i, 0))
    widths = (d_conv, d_conv, d_rwkv, d_rwkv, d_rwkv, d_z, d_g, d_model, d_model)
    dtypes = (BF16, BF16, F32, F32, F32, F32, F32, BF16, BF16)
    return pl.pallas_call(
        functools.partial(_in_proj_kernel, d_conv=d_conv, d_rwkv=d_rwkv, d_z=d_z, d_g=d_g,
                          d_model=d_model),
        grid=(n // tm,),
        in_specs=[row(d_model), _full((1, d_model)), _full((d_model, cols))],
        out_specs=[row(w) for w in widths],
        out_shape=[jax.ShapeDtypeStruct((n, w), dt) for w, dt in zip(widths, dtypes)],
        compiler_params=_params(1),
        name="in_proj",
    )(x, g, w_in)


def _wkv_masks():
    ti = lax.broadcasted_iota(jnp.int32, (CHUNK, PAIR), 0)
    lane = lax.broadcasted_iota(jnp.int32, (CHUNK, PAIR), 1)
    si = lane & (HEAD - 1)
    li = lax.broadcasted_iota(jnp.int32, (CHUNK, CHUNK), 0)
    lj = lax.broadcasted_iota(jnp.int32, (CHUNK, CHUNK), 1)
    head_lo = (lane < HEAD).astype(BF16)
    cum, aa, lvl0, lvls = [], [], [], []
    for reverse in (False, True):
        strict = ((si > ti) if reverse else (si < ti)).astype(BF16)
        incl = ((si >= ti) if reverse else (si <= ti)).astype(BF16)
        cum.append(((lj >= li) if reverse else (lj <= li)).astype(BF16))
        aa.append(jnp.concatenate([jnp.concatenate([strict, strict], axis=1),
                                   jnp.concatenate([incl, incl], axis=1)], axis=0))
        levels = []
        sz = 1
        while sz < CHUNK:
            same_blk = (ti & -(2 * sz)) == (si & -(2 * sz))
            t_hi, s_hi = (ti & sz) != 0, (si & sz) != 0
            levels.append((same_blk & ((~t_hi & s_hi) if reverse else (t_hi & ~s_hi))).astype(BF16))
            sz *= 2
        lvl0.append(levels[0])
        lvls.append(jnp.stack([jnp.concatenate([m * head_lo, m * (1 - head_lo)], axis=0)
                               for m in levels[1:]]))
    bi = lax.broadcasted_iota(jnp.int32, (PAIR, PAIR), 0)
    bj = lax.broadcasted_iota(jnp.int32, (PAIR, PAIR), 1)
    return dict(cum=jnp.stack(cum), aa=jnp.stack(aa), lvl0=jnp.stack(lvl0), lvls=jnp.stack(lvls),
                eye=(si == ti).astype(BF16), heads=jnp.stack([head_lo, 1 - head_lo]),
                same_head=((bi < HEAD) == (bj < HEAD)).astype(F32))


def _wkv_features(z_ref, mu, w2, zc_s, *, reverse, tb):
    z = z_ref[...]
    rows = lax.broadcasted_iota(jnp.int32, z.shape, 0)
    carry = jnp.broadcast_to(zc_s[0:1, :], z.shape)
    if reverse:
        zs = jnp.where(rows == tb - 1, carry, pltpu.roll(z, tb - 1, 0))
        zc_s[...] = jnp.broadcast_to(z[0:1, :], zc_s.shape)
    else:
        zs = jnp.where(rows == 0, carry, pltpu.roll(z, 1, 0))
        zc_s[...] = jnp.broadcast_to(z[tb - 1:tb, :], zc_s.shape)
    zm = z + mu * (zs - z)
    lanes = lax.broadcasted_iota(jnp.int32, z.shape, 1)
    feat = jnp.where(lanes < HEAD, jnp.tanh(zm), zm).astype(BF16)
    return _dot(feat, w2)


def _wkv_prep(lo, k, w0, a0, k_k, k_a, bd):
    d = k.shape[1]
    lw = -DECAY_SCALE * _sigmoid(w0 + lo[:, :d])
    a = _sigmoid(a0 + lo[:, d:])
    kkr = k * k_k
    kkn = kkr * lax.rsqrt(jnp.maximum(_head_sum(kkr * kkr, bd), 1e-24))
    return lw, kkn, kkn * a, k * (1.0 + (a - 1.0) * k_a)


def _wkv_kernel(rf_ref, kf_ref, vf_ref, zf_ref, rb_ref, kb_ref, vb_ref, zb_ref, mu_ref, w0_ref, a0_ref,
                w2_ref, kk_ref, ka_ref, rk_ref, bd_ref, cum_ref, aa_ref, lvl0_ref, lvls_ref, eye_ref,
                heads_ref, same_ref, yf_ref, yb_ref, bonus_ref, state_s, zc_s, *, tb, n_pairs):
    nch = tb // CHUNK
    r_refs, k_refs, v_refs, z_refs, y_refs = ((rf_ref, rb_ref), (kf_ref, kb_ref), (vf_ref, vb_ref),
                                              (zf_ref, zb_ref), (yf_ref, yb_ref))

    @pl.when(pl.program_id(1) == 0)
    def _():
        state_s[...] = jnp.zeros_like(state_s)
        zc_s[...] = jnp.zeros_like(zc_s)

    lo = [_wkv_features(z_refs[dr], mu_ref[dr], w2_ref[dr], zc_s.at[dr], reverse=bool(dr), tb=tb)
          for dr in range(2)]
    edge = (CHUNK - 1, 0)

    lane_lo = lax.broadcasted_iota(jnp.int32, (CHUNK, PAIR), 1) < HEAD

    def stack(x):
        if x.dtype == F32:
            return jnp.concatenate([jnp.where(lane_lo, x, 0.0), jnp.where(lane_lo, 0.0, x)],
                                   axis=0).astype(BF16)
        return jnp.concatenate([x * heads_ref[0], x * heads_ref[1]], axis=0)

    group = math.gcd(nch, CHUNKS_PER_STEP)
    span = group * CHUNK
    n_groups = nch // group
    units = [(dr, p, j) for j in range(group) for dr in range(2) for p in range(n_pairs)]
    halves = (units[:len(units) // 2], units[len(units) // 2:])
    n_levels = lvls_ref.shape[1] + 1
    state = [state_s[i] for i in range(2 * n_pairs)]

    def rows(g, dr, j):
        base = g * span if dr == 0 else tb - (g + 1) * span
        off = base + (j if dr == 0 else group - 1 - j) * CHUNK
        return slice(off, off + CHUNK)

    def head(g, out):
        prep = {}
        for dr in range(2):
            blk = slice(rows(g, dr, 0 if dr == 0 else group - 1).start,
                        rows(g, dr, group - 1 if dr == 0 else 0).stop)
            prep[dr] = _wkv_prep(lo[dr][blk], k_refs[dr][blk, :], w0_ref[dr], a0_ref[dr], kk_ref[...],
                                 ka_ref[...], bd_ref[...])
            if dr == 0:
                rk_sum = _head_sum(rf_ref[blk, :] * kf_ref[blk, :] * rk_ref[...], bd_ref[...])
                bonus_ref[blk, :] = (rk_sum * vf_ref[blk, :]).astype(BF16)
            yield

        def take(which, dr, p, j):
            off = (j if dr == 0 else group - 1 - j) * CHUNK
            return prep[dr][which][off:off + CHUNK, p * PAIR:(p + 1) * PAIR]

        def load(refs, dr, p, j):
            return refs[dr][rows(g, dr, j), p * PAIR:(p + 1) * PAIR]

        for key in ("cw", "rw", "kkw", "v", "a_ab", "a_rb", "a_kv", "tinv", "w_end", "bk_end"):
            out[key] = {}
        for part in halves:
            for un in part:
                dr, p, j = un
                h1, h2 = _split2(take(0, dr, p, j))
                cs = _dot(cum_ref[dr], jnp.concatenate([h1, h2], axis=1))
                out["cw"][un] = cs[:, :PAIR] + cs[:, PAIR:]
            yield
        for part in halves:
            for un in part:
                dr, p, j = un
                cw = out["cw"].pop(un)
                e_pos, e_neg = jnp.exp(cw), jnp.exp(-cw)
                rw = (load(r_refs, dr, p, j) * e_pos).astype(BF16)
                kkw = (take(1, dr, p, j) * jnp.exp(cw - take(0, dr, p, j))).astype(BF16)
                binv = take(2, dr, p, j) * e_neg
                kinv = take(3, dr, p, j) * e_neg
                w_end = e_pos[edge[dr]:edge[dr] + 1, :]
                out["rw"][un], out["kkw"][un], out["w_end"][un] = rw, kkw, w_end
                out["v"][un] = load(v_refs, dr, p, j).astype(BF16)
                out["bk_end"][un] = jnp.concatenate([binv * w_end, kinv * w_end], axis=0).astype(BF16)
                aa = (_dot_nt(jnp.concatenate([kkw, rw], axis=0),
                              jnp.concatenate([stack(binv), stack(kinv)], axis=0)).astype(BF16)
                      * aa_ref[dr])
                a_ab = aa[:CHUNK, :PAIR]
                out["a_ab"][un] = a_ab
                out["a_rb"][un] = aa[CHUNK:, :PAIR]
                out["a_kv"][un] = aa[:, PAIR:]
                out["tinv"][un] = eye_ref[...] - a_ab * lvl0_ref[dr]
            yield

    def tail(g, h):
        nonlocal state
        av, tt = {}, {}
        for part in halves:
            for un in part:
                av[un] = _dot(h["a_kv"][un], stack(h["v"][un]))
            yield
        for part in halves:
            for un in part:
                tt[un] = _dot(h["tinv"][un],
                              jnp.concatenate([stack(h["kkw"][un]), stack(av[un][:CHUNK])], axis=1))
            yield
        for jj in range(group):
            uns = [un for un in units if un[2] == jj]
            ps = [_dot_nt(jnp.concatenate([tt[un][:, :PAIR].astype(BF16), h["rw"][un]], axis=0),
                          s.astype(BF16)) for un, s in zip(uns, state)]
            yield
            u = [-(x[:CHUNK] + tt[un][:, PAIR:]) for un, x in zip(uns, ps)]
            y = [x[CHUNK:] + _dot(h["a_rb"][un], stack(w)) + av[un][CHUNK:]
                 for un, x, w in zip(uns, ps, u)]
            upd = [_dot_tn(jnp.concatenate([x.astype(BF16), h["v"][un]], axis=0), h["bk_end"][un])
                   for un, x in zip(uns, u)]
            yield
            for (dr, p, j), x in zip(uns, y):
                y_refs[dr][rows(g, dr, j), p * PAIR:(p + 1) * PAIR] = x.astype(BF16)
            state = [s * h["w_end"][un] + x * same_ref[...] for un, s, x in zip(uns, state, upd)]
            yield

    def level(h, lvl):
        sz = 1 << lvl
        m1, sel = {}, {}
        for un in units:
            dr = un[0]
            t = h["tinv"][un]
            if sz % BF16_ROWS == 0:
                sel[un] = [r0 for r0 in range(0, CHUNK, sz) if ((r0 & sz) != 0) != bool(dr)]
                t = jnp.concatenate([t[r0:r0 + sz] for r0 in sel[un]], axis=0)
            a2 = jnp.concatenate([h["a_ab"][un], h["a_ab"][un]], axis=0)
            m1[un] = _dot(t, a2 * lvls_ref[dr, lvl - 1])
        yield
        for un in units:
            t = h["tinv"][un]
            m2 = _dot(m1[un].astype(BF16), stack(t)).astype(BF16)
            if un in sel:
                blocks = [t[r0:r0 + sz] for r0 in range(0, CHUNK, sz)]
                for i, r0 in enumerate(sel[un]):
                    blocks[r0 // sz] = blocks[r0 // sz] - m2[i * sz:(i + 1) * sz]
                h["tinv"][un] = jnp.concatenate(blocks, axis=0)
            else:
                h["tinv"][un] = t - m2
        yield

    def advance(gens):
        for gen in gens:
            next(gen, None)

    def drain(gens):
        for gen in gens:
            for _ in gen:
                pass

    heads = [dict() for _ in range(n_groups)]
    drain([head(0, heads[0])])
    side = []
    for g in range(n_groups):
        h = heads[g]
        if g + 1 < n_groups:
            side.append(head(g + 1, heads[g + 1]))
        for lvl in range(1, n_levels):
            for _ in level(h, lvl):
                advance(side)
        drain(side)
        side = [tail(g, h)]
    drain(side)
    for i, s in enumerate(state):
        state_s[i] = s


def _wkv(r, k, v, z, mu, w0, a0, w2, k_k, k_a, r_k, bd, masks, *, bsz, seq, tb):
    n, d = r.shape
    nt = seq // tb
    n_pairs = d // PAIR
    fmap = lambda b, i: (b * nt + i, 0)
    bmap = lambda b, i: (b * nt + nt - 1 - i, 0)
    fblk, bblk = pl.BlockSpec((tb, d), fmap), pl.BlockSpec((tb, d), bmap)
    zf = pl.BlockSpec((tb, PAIR), fmap)
    zb = pl.BlockSpec((tb, PAIR), lambda b, i: (bmap(b, i)[0], 1))
    consts = [masks[key] for key in ("cum", "aa", "lvl0", "lvls", "eye", "heads", "same_head")]
    return pl.pallas_call(
        functools.partial(_wkv_kernel, tb=tb, n_pairs=n_pairs),
        grid=(bsz, nt),
        in_specs=[fblk, fblk, fblk, zf, bblk, bblk, bblk, zb,
                  _full(mu.shape), _full(w0.shape), _full(a0.shape), _full(w2.shape),
                  _full(k_k.shape), _full(k_a.shape), _full(r_k.shape), _full(bd.shape)]
        + [_full(m.shape) for m in consts],
        out_specs=[fblk, bblk, fblk],
        out_shape=[jax.ShapeDtypeStruct((n, d), BF16)] * 3,
        scratch_shapes=[pltpu.VMEM((2 * n_pairs, PAIR, PAIR), F32), pltpu.VMEM((2, SUBLANES, PAIR), F32)],
        compiler_params=_params(2),
        name="wkv",
    )(r, k, v, z, r, k, v, z, mu, w0, a0, w2, k_k, k_a, r_k, bd, *consts)


def _shift_matrix(rows):
    shape = (2 * rows, rows + 2 * BF16_ROWS)
    r = lax.broadcasted_iota(jnp.int32, shape, 0)
    c = lax.broadcasted_iota(jnp.int32, shape, 1)
    target = jnp.where(r < rows, r + (BF16_ROWS - 1), r - rows + (BF16_ROWS + 1))
    return (c == target).astype(BF16)


def _conv3(x_ref, prev_ref, next_ref, shift, w_ref, b_ref, cols, first, last, r0):
    m = x_ref.shape[0]
    rows = shift.shape[0] // 2
    lo, hi = r0 - BF16_ROWS, r0 + rows + BF16_ROWS
    pieces = [x_ref[max(lo, 0):min(hi, m), cols]]
    if lo < 0:
        halo = prev_ref[:, cols]
        pieces.insert(0, jnp.where(first, jnp.zeros_like(halo), halo))
    if hi > m:
        halo = next_ref[:, cols]
        pieces.append(jnp.where(last, jnp.zeros_like(halo), halo))
    window = pieces[0] if len(pieces) == 1 else jnp.concatenate(pieces, axis=0)
    sh = _dot(shift, window)
    x = x_ref[r0:r0 + rows, cols].astype(F32)
    return (sh[:rows] * w_ref[0:1, cols] + x * w_ref[1:2, cols] + sh[rows:] * w_ref[2:3, cols]
            + b_ref[:, cols])


def _halo_specs(tm, width, rows_per_blk, n_rows):
    nb = tm // rows_per_blk
    last_blk = n_rows // rows_per_blk - 1
    prev = pl.BlockSpec((rows_per_blk, width), lambda i: (jnp.maximum(i * nb - 1, 0), 0))
    nxt = pl.BlockSpec((rows_per_blk, width), lambda i: (jnp.minimum((i + 1) * nb, last_blk), 0))
    return prev, nxt


def _mix_out_kernel(x_ref, yf_ref, yb_ref, bonus_ref, gd_ref, c_ref, cp_ref, cn_ref, bg_ref,
                    sgc_ref, sgr_ref, cw_ref, cb_ref, wa_ref, g2_ref, gnw_ref, gnb_ref,
                    bd_ref, wb_ref, wo_ref, g_ref, o_ref, *, tm, seq):
    i = pl.program_id(0)
    first = (i * tm) % seq == 0
    last = ((i + 1) * tm) % seq == 0
    rows = min(CONV_ROWS, tm)
    shift = _shift_matrix(rows)
    conv = jnp.concatenate([_conv3(c_ref, cp_ref, cn_ref, shift, cw_ref, cb_ref, slice(None), first,
                                   last, r0) for r0 in range(0, tm, rows)], axis=0)
    y_conv = _dot((bg_ref[...].astype(F32) * conv).astype(BF16), wa_ref[...])
    bd = bd_ref[...]
    y = yf_ref[...].astype(F32) + yb_ref[...].astype(F32)
    mean = _head_sum(y, bd) * (1.0 / HEAD)
    yc = y - mean
    var = _head_sum(yc * yc, bd) * (1.0 / HEAD)
    yn = yc * lax.rsqrt(var + GN_EPS) * gnw_ref[...] + gnb_ref[...]
    gate = _dot(_sigmoid(gd_ref[...]).astype(BF16), g2_ref[...])
    y_rwkv = _dot(((yn + bonus_ref[...].astype(F32)) * gate).astype(BF16), wb_ref[...])
    merged = sgc_ref[...].astype(F32) * y_conv + sgr_ref[...].astype(F32) * y_rwkv
    m = _dot(merged.astype(BF16), wo_ref[...])
    o_ref[...] = x_ref[...] + _rms(m, g_ref[...])


def _mix_out(x, yf, yb, bonus, gd, cghc, bgate, sgc, sgr, conv_w, conv_b, w_a, g2, gn_w, gn_b,
             bd, w_b, w_out, g_post, *, tm, seq):
    n, d_model = x.shape
    d = yf.shape[1]
    d_conv = cghc.shape[1]
    row = lambda w: pl.BlockSpec((tm, w), lambda i: (i, 0))
    cp, cn = _halo_specs(tm, d_conv, BF16_ROWS, n)
    return pl.pallas_call(
        functools.partial(_mix_out_kernel, tm=tm, seq=seq),
        grid=(n // tm,),
        in_specs=[row(d_model), row(d), row(d), row(d), row(gd.shape[1]),
                  row(d_conv), cp, cn, row(d_conv), row(d_model), row(d_model),
                  _full(conv_w.shape), _full(conv_b.shape), _full(w_a.shape), _full(g2.shape),
                  _full(gn_w.shape), _full(gn_b.shape), _full(bd.shape),
                  _full(w_b.shape), _full(w_out.shape), _full(g_post.shape)],
        out_specs=row(d_model),
        out_shape=jax.ShapeDtypeStruct((n, d_model), F32),
        compiler_params=_params(1),
        name="mix_out",
    )(x, yf, yb, bonus, gd, cghc, cghc, cghc, bgate, sgc, sgr, conv_w, conv_b, w_a, g2, gn_w,
      gn_b, bd, w_b, w_out, g_post)


def _ffn_kernel(x_ref, xp_ref, xn_ref, p_ref, gpre_ref, wu_ref, cw_ref, cb_ref, wd_ref, gf_ref, wp_ref,
                wg_ref, gp_ref, o_ref, act_s, *, tm, seq, d_ff, col_chunk):
    i = pl.program_id(0)
    first = (i * tm) % seq == 0
    last = ((i + 1) * tm) % seq == 0
    g = gpre_ref[...]
    u_prev = jnp.where(first, 0.0, _rms(xp_ref[...], g))
    u_next = jnp.where(last, 0.0, _rms(xn_ref[...], g))
    u = jnp.concatenate([u_prev, _rms(x_ref[...], g), u_next], axis=0).astype(BF16)
    m = tm + 2 * SUBLANES
    for c0 in range(0, d_ff, col_chunk):
        conv = []
        for off in (c0, d_ff + c0):
            cols = slice(off, off + col_chunk)
            h = _dot(u, wu_ref[:, cols])
            conv.append(pltpu.roll(h, 1, 0) * cw_ref[0:1, cols] + h * cw_ref[1:2, cols]
                        + pltpu.roll(h, m - 1, 0) * cw_ref[2:3, cols] + cb_ref[:, cols])
        hg, hv = conv
        gelu = 0.5 * hg * (1.0 + jnp.tanh(GELU_C * (hg + 0.044715 * (hg * hg * hg))))
        act_s[:, c0:c0 + col_chunk] = (gelu * hv)[SUBLANES:SUBLANES + tm].astype(BF16)
    f = _dot(act_s[...], wd_ref[...])
    x = x_ref[...] + _rms(f, gf_ref[...])
    gate = _sigmoid(_dot(x.astype(BF16), wg_ref[...]))
    pe = _dot(p_ref[...].astype(BF16), wp_ref[...])
    o_ref[...] = x + _rms(gate * pe, gp_ref[...])


def _ffn(x, p, g_pre, w_up, conv_w, conv_b, w_down, g_ffn, w_ple, w_gate, g_ple, *, layer, tm, seq,
         col_chunk):
    n, d_model = x.shape
    d_ff = w_down.shape[0]
    row = lambda w: pl.BlockSpec((tm, w), lambda i: (i, 0))
    xp, xn = _halo_specs(tm, d_model, SUBLANES, n)
    const = lambda a: pl.BlockSpec(a.shape, lambda i: (0,) * a.ndim, pipeline_mode=pl.Buffered(1))
    return pl.pallas_call(
        functools.partial(_ffn_kernel, tm=tm, seq=seq, d_ff=d_ff, col_chunk=col_chunk),
        grid=(n // tm,),
        in_specs=[row(d_model), xp, xn, pl.BlockSpec((None, tm, p.shape[2]), lambda i: (layer, i, 0)),
                  const(g_pre), const(w_up), const(conv_w),
                  const(conv_b), const(w_down), const(g_ffn), const(w_ple), const(w_gate), const(g_ple)],
        out_specs=row(d_model),
        out_shape=jax.ShapeDtypeStruct((n, d_model), F32),
        scratch_shapes=[pltpu.VMEM((tm, d_ff), BF16)],
        compiler_params=_params(1),
        name="ffn",
    )(x, x, x, p, g_pre, w_up, conv_w, conv_b, w_down, g_ffn, w_ple, w_gate, g_ple)


def _tiles(seq):
    tm = min(256, seq)
    tb = min(512, seq)
    return tm, tb


def _layer_weights(i, norm_mix_pre, norm_mix_post, norm_ffn_pre, norm_ffn_post, norm_ple_post, w_in,
                   conv_w, conv_b, w_branch_a, shift_mu, decay_w0, decay_w2, iclr_a0, iclr_a2, gate_g2,
                   k_k, k_a, r_k, gn_w, gn_b, w_branch_b, w_out, w_up, ffn_conv_w, ffn_conv_b, w_down,
                   w_ple, w_ple_gate):
    d_rwkv = k_k.shape[1]
    lora = decay_w2.shape[2]
    row = lambda a: a[i].reshape(1, -1)
    zeros = jnp.zeros((lora, d_rwkv), F32)
    lowrank = [jnp.concatenate([jnp.concatenate([decay_w2[i, d], zeros], axis=1),
                                jnp.concatenate([zeros, iclr_a2[i, d]], axis=1)], axis=0).astype(BF16)
               for d in range(2)]
    head_id = jnp.arange(d_rwkv) // HEAD
    return dict(
        g_mix_pre=row(norm_mix_pre), g_mix_post=row(norm_mix_post), g_ffn_pre=row(norm_ffn_pre),
        g_ffn_post=row(norm_ffn_post), g_ple_post=row(norm_ple_post),
        w_in=w_in[i].astype(BF16), conv_w=conv_w[i], conv_b=row(conv_b),
        w_a=w_branch_a[i].astype(BF16),
        mu=shift_mu[i][:, None, :], w0=decay_w0[i][:, None, :], a0=iclr_a0[i][:, None, :],
        lowrank=jnp.stack(lowrank), g2=gate_g2[i].astype(BF16), k_k=row(k_k), k_a=row(k_a),
        r_k=r_k[i].reshape(1, -1), gn_w=row(gn_w), gn_b=row(gn_b),
        bd=(head_id[:, None] == head_id[None, :]).astype(BF16),
        w_b=w_branch_b[i].astype(BF16), w_out=w_out[i].astype(BF16), w_up=w_up[i].astype(BF16),
        ffn_conv_w=ffn_conv_w[i], ffn_conv_b=row(ffn_conv_b), w_down=w_down[i].astype(BF16),
        w_ple=w_ple[i].astype(BF16), w_gate=w_ple_gate[i].astype(BF16))


def _layer(x, p, lw, masks, *, layer, bsz, seq):
    tm, tb = _tiles(seq)
    d_conv = lw["conv_w"].shape[1]
    d_rwkv = lw["k_k"].shape[1]
    d_z = lw["mu"].shape[2]
    d_g = lw["g2"].shape[0]
    d_ff = lw["w_down"].shape[0]
    cghc, bgate, r, k, v, z, gd, sgc, sgr = _in_proj(
        x, lw["g_mix_pre"], lw["w_in"], tm=tm, d_conv=d_conv, d_rwkv=d_rwkv, d_z=2 * d_z, d_g=d_g)
    yf, yb, bonus = _wkv(r, k, v, z, lw["mu"], lw["w0"], lw["a0"], lw["lowrank"], lw["k_k"], lw["k_a"],
                         lw["r_k"], lw["bd"], masks, bsz=bsz, seq=seq, tb=tb)
    x = _mix_out(x, yf, yb, bonus, gd, cghc, bgate, sgc, sgr, lw["conv_w"], lw["conv_b"],
                 lw["w_a"], lw["g2"], lw["gn_w"], lw["gn_b"], lw["bd"], lw["w_b"],
                 lw["w_out"], lw["g_mix_post"], tm=tm, seq=seq)
    return _ffn(x, p, lw["g_ffn_pre"], lw["w_up"], lw["ffn_conv_w"], lw["ffn_conv_b"], lw["w_down"],
                lw["g_ffn_post"], lw["w_ple"], lw["w_gate"], lw["g_ple_post"], layer=layer, tm=tm,
                seq=seq, col_chunk=math.gcd(d_ff, 256))


def kernel(x_prompt, x_sample, p_prompt, p_sample, norm_mix_pre, norm_mix_post, norm_ffn_pre, norm_ffn_post, norm_ple_post, w_in, conv_w, conv_b, w_branch_a, shift_mu, decay_w0, decay_w2, iclr_a0, iclr_a2, gate_g2, k_k, k_a, r_k, gn_w, gn_b, w_branch_b, w_out, w_up, ffn_conv_w, ffn_conv_b, w_down, w_ple, w_ple_gate):
    weights = (norm_mix_pre, norm_mix_post, norm_ffn_pre, norm_ffn_post, norm_ple_post, w_in, conv_w,
               conv_b, w_branch_a, shift_mu, decay_w0, decay_w2, iclr_a0, iclr_a2, gate_g2, k_k, k_a,
               r_k, gn_w, gn_b, w_branch_b, w_out, w_up, ffn_conv_w, ffn_conv_b, w_down, w_ple,
               w_ple_gate)
    layers = [_layer_weights(i, *weights) for i in range(w_in.shape[0])]
    masks = _wkv_masks()
    outs = []
    for x, p in ((x_prompt, p_prompt), (x_sample, p_sample)):
        bsz, seq, d_model = x.shape
        y = x.reshape(bsz * seq, d_model)
        p = p.reshape(p.shape[0], bsz * seq, -1)
        for i, lw in enumerate(layers):
            y = _layer(y, p, lw, masks, layer=i, bsz=bsz, seq=seq)
        outs.append(y.reshape(bsz, seq, d_model))
    return tuple(outs)
```

```python
import functools
import math

import jax
import jax.numpy as jnp
from jax import lax
from jax.experimental import pallas as pl
from jax.experimental.pallas import tpu as pltpu

F32 = jnp.float32
BF16 = jnp.bfloat16

MXU_WIDTH = 256
SUBLANES = 8
BF16_ROWS = 16
VMEM_LIMIT_BYTES = 56 * 1024 * 1024

HEAD = 64
PAIR = 2 * HEAD
CHUNK = 64
CHUNKS_PER_STEP = 2
CONV_ROWS = 128
NORM_EPS = 1e-6
GN_EPS = HEAD * 1e-5
DECAY_SCALE = math.exp(-0.5)
GELU_C = math.sqrt(2.0 / math.pi)


def _sigmoid(x):
    return 1.0 / (1.0 + jnp.exp(-x))


def _rms(x, g):
    return x * lax.rsqrt(jnp.mean(x * x, axis=-1, keepdims=True) + NORM_EPS) * g


def _dot(a, b):
    return jnp.dot(a, b, preferred_element_type=F32)


def _dot_nt(a, b):
    return lax.dot_general(a, b, (((1,), (1,)), ((), ())), preferred_element_type=F32)


def _dot_tn(a, b):
    return lax.dot_general(a, b, (((0,), (0,)), ((), ())), preferred_element_type=F32)


def _split2(x):
    hi = x.astype(BF16)
    lo = (x - hi.astype(F32)).astype(BF16)
    return hi, lo


def _head_sum(x, bd):
    return _dot(x.astype(BF16), bd)


def _params(n_axes):
    return pltpu.CompilerParams(dimension_semantics=("arbitrary",) * n_axes,
                                vmem_limit_bytes=VMEM_LIMIT_BYTES)


def _full(shape):
    nd = len(shape)
    return pl.BlockSpec(shape, lambda *_: (0,) * nd)


def _in_proj_kernel(x_ref, g_ref, w_ref, cghc_ref, bg_ref, r_ref, k_ref, v_ref, z_ref, gd_ref,
                    sgc_ref, sgr_ref, *, d_conv, d_rwkv, d_z, d_g, d_model):
    u = _rms(x_ref[...], g_ref[...]).astype(BF16)
    o = 0
    hbc = _dot(u, w_ref[:, o:o + 3 * d_conv])
    cghc_ref[...] = (hbc[:, 2 * d_conv:] * hbc[:, :d_conv]).astype(BF16)
    bg_ref[...] = hbc[:, d_conv:2 * d_conv].astype(BF16)
    o += 3 * d_conv
    for ref in (r_ref, k_ref, v_ref):
        ref[...] = _dot(u, w_ref[:, o:o + d_rwkv])
        o += d_rwkv
    z_ref[...] = _dot(u, w_ref[:, o:o + d_z])
    o += d_z
    gd_ref[...] = _dot(u, w_ref[:, o:o + d_g])
    o += d_g
    for ref in (sgc_ref, sgr_ref):
        ref[...] = _sigmoid(_dot(u, w_ref[:, o:o + d_model])).astype(BF16)
        o += d_model


def _in_proj(x, g, w_in, *, tm, d_conv, d_rwkv, d_z, d_g):
    n, d_model = x.shape
    cols = w_in.shape[1]
    row = lambda w: pl.BlockSpec((tm, w), lambda i: (i, 0))
    widths = (d_conv, d_conv, d_rwkv, d_rwkv, d_rwkv, d_z, d_g, d_model, d_model)
    dtypes = (BF16, BF16, F32, F32, F32, F32, F32, BF16, BF16)
    return pl.pallas_call(
        functools.partial(_in_proj_kernel, d_conv=d_conv, d_rwkv=d_rwkv, d_z=d_z, d_g=d_g,
                          d_model=d_model),
        grid=(n // tm,),
        in_specs=[row(d_model), _full((1, d_model)), _full((d_model, cols))],
        out_specs=[row(w) for w in widths],
        out_shape=[jax.ShapeDtypeStruct((n, w), dt) for w, dt in zip(widths, dtypes)],
        compiler_params=_params(1),
        name="in_proj",
    )(x, g, w_in)


def _wkv_masks():
    ti = lax.broadcasted_iota(jnp.int32, (CHUNK, PAIR), 0)
    lane = lax.broadcasted_iota(jnp.int32, (CHUNK, PAIR), 1)
    si = lane & (HEAD - 1)
    li = lax.broadcasted_iota(jnp.int32, (CHUNK, CHUNK), 0)
    lj = lax.broadcasted_iota(jnp.int32, (CHUNK, CHUNK), 1)
    head_lo = (lane < HEAD).astype(BF16)
    cum, aa, lvl0, lvls = [], [], [], []
    for reverse in (False, True):
        strict = ((si > ti) if reverse else (si < ti)).astype(BF16)
        incl = ((si >= ti) if reverse else (si <= ti)).astype(BF16)
        cum.append(((lj >= li) if reverse else (lj <= li)).astype(BF16))
        aa.append(jnp.concatenate([jnp.concatenate([strict, strict], axis=1),
                                   jnp.concatenate([incl, incl], axis=1)], axis=0))
        levels = []
        sz = 1
        while sz < CHUNK:
            same_blk = (ti & -(2 * sz)) == (si & -(2 * sz))
            t_hi, s_hi = (ti & sz) != 0, (si & sz) != 0
            levels.append((same_blk & ((~t_hi & s_hi) if reverse else (t_hi & ~s_hi))).astype(BF16))
            sz *= 2
        lvl0.append(levels[0])
        lvls.append(jnp.stack([jnp.concatenate([m * head_lo, m * (1 - head_lo)], axis=0)
                               for m in levels[1:]]))
    bi = lax.broadcasted_iota(jnp.int32, (PAIR, PAIR), 0)
    bj = lax.broadcasted_iota(jnp.int32, (PAIR, PAIR), 1)
    return dict(cum=jnp.stack(cum), aa=jnp.stack(aa), lvl0=jnp.stack(lvl0), lvls=jnp.stack(lvls),
                eye=(si == ti).astype(BF16), heads=jnp.stack([head_lo, 1 - head_lo]),
                same_head=((bi < HEAD) == (bj < HEAD)).astype(F32))


def _wkv_features(z_ref, mu, w2, zc_s, *, reverse, tb):
    z = z_ref[...]
    rows = lax.broadcasted_iota(jnp.int32, z.shape, 0)
    carry = jnp.broadcast_to(zc_s[0:1, :], z.shape)
    if reverse:
        zs = jnp.where(rows == tb - 1, carry, pltpu.roll(z, tb - 1, 0))
        zc_s[...] = jnp.broadcast_to(z[0:1, :], zc_s.shape)
    else:
        zs = jnp.where(rows == 0, carry, pltpu.roll(z, 1, 0))
        zc_s[...] = jnp.broadcast_to(z[tb - 1:tb, :], zc_s.shape)
    zm = z + mu * (zs - z)
    lanes = lax.broadcasted_iota(jnp.int32, z.shape, 1)
    feat = jnp.where(lanes < HEAD, jnp.tanh(zm), zm).astype(BF16)
    return _dot(feat, w2)


def _wkv_prep(lo, k, w0, a0, k_k, k_a, bd):
    d = k.shape[1]
    lw = -DECAY_SCALE * _sigmoid(w0 + lo[:, :d])
    a = _sigmoid(a0 + lo[:, d:])
    kkr = k * k_k
    kkn = kkr * lax.rsqrt(jnp.maximum(_head_sum(kkr * kkr, bd), 1e-24))
    return lw, kkn, kkn * a, k * (1.0 + (a - 1.0) * k_a)


def _wkv_kernel(rf_ref, kf_ref, vf_ref, zf_ref, rb_ref, kb_ref, vb_ref, zb_ref, mu_ref, w0_ref, a0_ref,
                w2_ref, kk_ref, ka_ref, rk_ref, bd_ref, cum_ref, aa_ref, lvl0_ref, lvls_ref, eye_ref,
                heads_ref, same_ref, yf_ref, yb_ref, bonus_ref, state_s, zc_s, *, tb, n_pairs):
    nch = tb // CHUNK
    r_refs, k_refs, v_refs, z_refs, y_refs = ((rf_ref, rb_ref), (kf_ref, kb_ref), (vf_ref, vb_ref),
                                              (zf_ref, zb_ref), (yf_ref, yb_ref))

    @pl.when(pl.program_id(1) == 0)
    def _():
        state_s[...] = jnp.zeros_like(state_s)
        zc_s[...] = jnp.zeros_like(zc_s)

    lo = [_wkv_features(z_refs[dr], mu_ref[dr], w2_ref[dr], zc_s.at[dr], reverse=bool(dr), tb=tb)
          for dr in range(2)]
    edge = (CHUNK - 1, 0)

    lane_lo = lax.broadcasted_iota(jnp.int32, (CHUNK, PAIR), 1) < HEAD

    def stack(x):
        if x.dtype == F32:
            return jnp.concatenate([jnp.where(lane_lo, x, 0.0), jnp.where(lane_lo, 0.0, x)],
                                   axis=0).astype(BF16)
        return jnp.concatenate([x * heads_ref[0], x * heads_ref[1]], axis=0)

    group = math.gcd(nch, CHUNKS_PER_STEP)
    span = group * CHUNK
    n_groups = nch // group
    units = [(dr, p, j) for j in range(group) for dr in range(2) for p in range(n_pairs)]
    halves = (units[:len(units) // 2], units[len(units) // 2:])
    n_levels = lvls_ref.shape[1] + 1
    state = [state_s[i] for i in range(2 * n_pairs)]

    def rows(g, dr, j):
        base = g * span if dr == 0 else tb - (g + 1) * span
        off = base + (j if dr == 0 else group - 1 - j) * CHUNK
        return slice(off, off + CHUNK)

    def head(g, out):
        prep = {}
        for dr in range(2):
            blk = slice(rows(g, dr, 0 if dr == 0 else group - 1).start,
                        rows(g, dr, group - 1 if dr == 0 else 0).stop)
            prep[dr] = _wkv_prep(lo[dr][blk], k_refs[dr][blk, :], w0_ref[dr], a0_ref[dr], kk_ref[...],
                                 ka_ref[...], bd_ref[...])
            if dr == 0:
                rk_sum = _head_sum(rf_ref[blk, :] * kf_ref[blk, :] * rk_ref[...], bd_ref[...])
                bonus_ref[blk, :] = (rk_sum * vf_ref[blk, :]).astype(BF16)
            yield

        def take(which, dr, p, j):
            off = (j if dr == 0 else group - 1 - j) * CHUNK
            return prep[dr][which][off:off + CHUNK, p * PAIR:(p + 1) * PAIR]

        def load(refs, dr, p, j):
            return refs[dr][rows(g, dr, j), p * PAIR:(p + 1) * PAIR]

        for key in ("cw", "rw", "kkw", "v", "a_ab", "a_rb", "a_kv", "tinv", "w_end", "bk_end"):
            out[key] = {}
        for part in halves:
            for un in part:
                dr, p, j = un
                h1, h2 = _split2(take(0, dr, p, j))
                cs = _dot(cum_ref[dr], jnp.concatenate([h1, h2], axis=1))
                out["cw"][un] = cs[:, :PAIR] + cs[:, PAIR:]
            yield
        for part in halves:
            for un in part:
                dr, p, j = un
                cw = out["cw"].pop(un)
                e_pos, e_neg = jnp.exp(cw), jnp.exp(-cw)
                rw = (load(r_refs, dr, p, j) * e_pos).astype(BF16)
                kkw = (take(1, dr, p, j) * jnp.exp(cw - take(0, dr, p, j))).astype(BF16)
                binv = take(2, dr, p, j) * e_neg
                kinv = take(3, dr, p, j) * e_neg
                w_end = e_pos[edge[dr]:edge[dr] + 1, :]
                out["rw"][un], out["kkw"][un], out["w_end"][un] = rw, kkw, w_end
                out["v"][un] = load(v_refs, dr, p, j).astype(BF16)
                out["bk_end"][un] = jnp.concatenate([binv * w_end, kinv * w_end], axis=0).astype(BF16)
                aa = (_dot_nt(jnp.concatenate([kkw, rw], axis=0),
                              jnp.concatenate([stack(binv), stack(kinv)], axis=0)).astype(BF16)
                      * aa_ref[dr])
                a_ab = aa[:CHUNK, :PAIR]
                out["a_ab"][un] = a_ab
                out["a_rb"][un] = aa[CHUNK:, :PAIR]
                out["a_kv"][un] = aa[:, PAIR:]
                out["tinv"][un] = eye_ref[...] - a_ab * lvl0_ref[dr]
            yield

    def tail(g, h):
        nonlocal state
        av, tt = {}, {}
        for part in halves:
            for un in part:
                av[un] = _dot(h["a_kv"][un], stack(h["v"][un]))
            yield
        for part in halves:
            for un in part:
                tt[un] = _dot(h["tinv"][un],
                              jnp.concatenate([stack(h["kkw"][un]), stack(av[un][:CHUNK])], axis=1))
            yield
        for jj in range(group):
            uns = [un for un in units if un[2] == jj]
            ps = [_dot_nt(jnp.concatenate([tt[un][:, :PAIR].astype(BF16), h["rw"][un]], axis=0),
                          s.astype(BF16)) for un, s in zip(uns, state)]
            yield
            u = [-(x[:CHUNK] + tt[un][:, PAIR:]) for un, x in zip(uns, ps)]
            y = [x[CHUNK:] + _dot(h["a_rb"][un], stack(w)) + av[un][CHUNK:]
                 for un, x, w in zip(uns, ps, u)]
            upd = [_dot_tn(jnp.concatenate([x.astype(BF16), h["v"][un]], axis=0), h["bk_end"][un])
                   for un, x in zip(uns, u)]
            yield
            for (dr, p, j), x in zip(uns, y):
                y_refs[dr][rows(g, dr, j), p * PAIR:(p + 1) * PAIR] = x.astype(BF16)
            state = [s * h["w_end"][un] + x * same_ref[...] for un, s, x in zip(uns, state, upd)]
            yield

    def level(h, lvl):
        sz = 1 << lvl
        m1, sel = {}, {}
        for un in units:
            dr = un[0]
            t = h["tinv"][un]
            if sz % BF16_ROWS == 0:
                sel[un] = [r0 for r0 in range(0, CHUNK, sz) if ((r0 & sz) != 0) != bool(dr)]
                t = jnp.concatenate([t[r0:r0 + sz] for r0 in sel[un]], axis=0)
            a2 = jnp.concatenate([h["a_ab"][un], h["a_ab"][un]], axis=0)
            m1[un] = _dot(t, a2 * lvls_ref[dr, lvl - 1])
        yield
        for un in units:
            t = h["tinv"][un]
            m2 = _dot(m1[un].astype(BF16), stack(t)).astype(BF16)
            if un in sel:
                blocks = [t[r0:r0 + sz] for r0 in range(0, CHUNK, sz)]
                for i, r0 in enumerate(sel[un]):
                    blocks[r0 // sz] = blocks[r0 // sz] - m2[i * sz:(i + 1) * sz]
                h["tinv"][un] = jnp.concatenate(blocks, axis=0)
            else:
                h["tinv"][un] = t - m2
        yield

    def advance(gens):
        for gen in gens:
            next(gen, None)

    def drain(gens):
        for gen in gens:
            for _ in gen:
                pass

    heads = [dict() for _ in range(n_groups)]
    drain([head(0, heads[0])])
    side = []
    for g in range(n_groups):
        h = heads[g]
        if g + 1 < n_groups:
            side.append(head(g + 1, heads[g + 1]))
        for lvl in range(1, n_levels):
            for _ in level(h, lvl):
                advance(side)
        drain(side)
        side = [tail(g, h)]
    drain(side)
    for i, s in enumerate(state):
        state_s[i] = s


def _wkv(r, k, v, z, mu, w0, a0, w2, k_k, k_a, r_k, bd, masks, *, bsz, seq, tb):
    n, d = r.shape
    nt = seq // tb
    n_pairs = d // PAIR
    fmap = lambda b, i: (b * nt + i, 0)
    bmap = lambda b, i: (b * nt + nt - 1 - i, 0)
    fblk, bblk = pl.BlockSpec((tb, d), fmap), pl.BlockSpec((tb, d), bmap)
    zf = pl.BlockSpec((tb, PAIR), fmap)
    zb = pl.BlockSpec((tb, PAIR), lambda b, i: (bmap(b, i)[0], 1))
    consts = [masks[key] for key in ("cum", "aa", "lvl0", "lvls", "eye", "heads", "same_head")]
    return pl.pallas_call(
        functools.partial(_wkv_kernel, tb=tb, n_pairs=n_pairs),
        grid=(bsz, nt),
        in_specs=[fblk, fblk, fblk, zf, bblk, bblk, bblk, zb,
                  _full(mu.shape), _full(w0.shape), _full(a0.shape), _full(w2.shape),
                  _full(k_k.shape), _full(k_a.shape), _full(r_k.shape), _full(bd.shape)]
        + [_full(m.shape) for m in consts],
        out_specs=[fblk, bblk, fblk],
        out_shape=[jax.ShapeDtypeStruct((n, d), BF16)] * 3,
        scratch_shapes=[pltpu.VMEM((2 * n_pairs, PAIR, PAIR), F32), pltpu.VMEM((2, SUBLANES, PAIR), F32)],
        compiler_params=_params(2),
        name="wkv",
    )(r, k, v, z, r, k, v, z, mu, w0, a0, w2, k_k, k_a, r_k, bd, *consts)


def _shift_matrix(rows):
    shape = (2 * rows, rows + 2 * BF16_ROWS)
    r = lax.broadcasted_iota(jnp.int32, shape, 0)
    c = lax.broadcasted_iota(jnp.int32, shape, 1)
    target = jnp.where(r < rows, r + (BF16_ROWS - 1), r - rows + (BF16_ROWS + 1))
    return (c == target).astype(BF16)


def _conv3(x_ref, prev_ref, next_ref, shift, w_ref, b_ref, cols, first, last, r0):
    m = x_ref.shape[0]
    rows = shift.shape[0] // 2
    lo, hi = r0 - BF16_ROWS, r0 + rows + BF16_ROWS
    pieces = [x_ref[max(lo, 0):min(hi, m), cols]]
    if lo < 0:
        halo = prev_ref[:, cols]
        pieces.insert(0, jnp.where(first, jnp.zeros_like(halo), halo))
    if hi > m:
        halo = next_ref[:, cols]
        pieces.append(jnp.where(last, jnp.zeros_like(halo), halo))
    window = pieces[0] if len(pieces) == 1 else jnp.concatenate(pieces, axis=0)
    sh = _dot(shift, window)
    x = x_ref[r0:r0 + rows, cols].astype(F32)
    return (sh[:rows] * w_ref[0:1, cols] + x * w_ref[1:2, cols] + sh[rows:] * w_ref[2:3, cols]
            + b_ref[:, cols])


def _halo_specs(tm, width, rows_per_blk, n_rows):
    nb = tm // rows_per_blk
    last_blk = n_rows // rows_per_blk - 1
    prev = pl.BlockSpec((rows_per_blk, width), lambda i: (jnp.maximum(i * nb - 1, 0), 0))
    nxt = pl.BlockSpec((rows_per_blk, width), lambda i: (jnp.minimum((i + 1) * nb, last_blk), 0))
    return prev, nxt


def _mix_out_kernel(x_ref, yf_ref, yb_ref, bonus_ref, gd_ref, c_ref, cp_ref, cn_ref, bg_ref,
                    sgc_ref, sgr_ref, cw_ref, cb_ref, wa_ref, g2_ref, gnw_ref, gnb_ref,
                    bd_ref, wb_ref, wo_ref, g_ref, o_ref, *, tm, seq):
    i = pl.program_id(0)
    first = (i * tm) % seq == 0
    last = ((i + 1) * tm) % seq == 0
    rows = min(CONV_ROWS, tm)
    shift = _shift_matrix(rows)
    conv = jnp.concatenate([_conv3(c_ref, cp_ref, cn_ref, shift, cw_ref, cb_ref, slice(None), first,
                                   last, r0) for r0 in range(0, tm, rows)], axis=0)
    y_conv = _dot((bg_ref[...].astype(F32) * conv).astype(BF16), wa_ref[...])
    bd = bd_ref[...]
    y = yf_ref[...].astype(F32) + yb_ref[...].astype(F32)
    mean = _head_sum(y, bd) * (1.0 / HEAD)
    yc = y - mean
    var = _head_sum(yc * yc, bd) * (1.0 / HEAD)
    yn = yc * lax.rsqrt(var + GN_EPS) * gnw_ref[...] + gnb_ref[...]
    gate = _dot(_sigmoid(gd_ref[...]).astype(BF16), g2_ref[...])
    y_rwkv = _dot(((yn + bonus_ref[...].astype(F32)) * gate).astype(BF16), wb_ref[...])
    merged = sgc_ref[...].astype(F32) * y_conv + sgr_ref[...].astype(F32) * y_rwkv
    m = _dot(merged.astype(BF16), wo_ref[...])
    o_ref[...] = x_ref[...] + _rms(m, g_ref[...])


def _mix_out(x, yf, yb, bonus, gd, cghc, bgate, sgc, sgr, conv_w, conv_b, w_a, g2, gn_w, gn_b,
             bd, w_b, w_out, g_post, *, tm, seq):
    n, d_model = x.shape
    d = yf.shape[1]
    d_conv = cghc.shape[1]
    row = lambda w: pl.BlockSpec((tm, w), lambda i: (i, 0))
    cp, cn = _halo_specs(tm, d_conv, BF16_ROWS, n)
    return pl.pallas_call(
        functools.partial(_mix_out_kernel, tm=tm, seq=seq),
        grid=(n // tm,),
        in_specs=[row(d_model), row(d), row(d), row(d), row(gd.shape[1]),
                  row(d_conv), cp, cn, row(d_conv), row(d_model), row(d_model),
                  _full(conv_w.shape), _full(conv_b.shape), _full(w_a.shape), _full(g2.shape),
                  _full(gn_w.shape), _full(gn_b.shape), _full(bd.shape),
                  _full(w_b.shape), _full(w_out.shape), _full(g_post.shape)],
        out_specs=row(d_model),
        out_shape=jax.ShapeDtypeStruct((n, d_model), F32),
        compiler_params=_params(1),
        name="mix_out",
    )(x, yf, yb, bonus, gd, cghc, cghc, cghc, bgate, sgc, sgr, conv_w, conv_b, w_a, g2, gn_w,
      gn_b, bd, w_b, w_out, g_post)


def _ffn_kernel(x_ref, xp_ref, xn_ref, p_ref, gpre_ref, wu_ref, cw_ref, cb_ref, wd_ref, gf_ref, wp_ref,
                wg_ref, gp_ref, o_ref, act_s, *, tm, seq, d_ff, col_chunk):
    i = pl.program_id(0)
    first = (i * tm) % seq == 0
    last = ((i + 1) * tm) % seq == 0
    g = gpre_ref[...]
    u_prev = jnp.where(first, 0.0, _rms(xp_ref[...], g))
    u_next = jnp.where(last, 0.0, _rms(xn_ref[...], g))
    u = jnp.concatenate([u_prev, _rms(x_ref[...], g), u_next], axis=0).astype(BF16)
    m = tm + 2 * SUBLANES
    for c0 in range(0, d_ff, col_chunk):
        conv = []
        for off in (c0, d_ff + c0):
            cols = slice(off, off + col_chunk)
            h = _dot(u, wu_ref[:, cols])
            conv.append(pltpu.roll(h, 1, 0) * cw_ref[0:1, cols] + h * cw_ref[1:2, cols]
                        + pltpu.roll(h, m - 1, 0) * cw_ref[2:3, cols] + cb_ref[:, cols])
        hg, hv = conv
        gelu = 0.5 * hg * (1.0 + jnp.tanh(GELU_C * (hg + 0.044715 * (hg * hg * hg))))
        act_s[:, c0:c0 + col_chunk] = (gelu * hv)[SUBLANES:SUBLANES + tm].astype(BF16)
    f = _dot(act_s[...], wd_ref[...])
    x = x_ref[...] + _rms(f, gf_ref[...])
    gate = _sigmoid(_dot(x.astype(BF16), wg_ref[...]))
    pe = _dot(p_ref[...].astype(BF16), wp_ref[...])
    o_ref[...] = x + _rms(gate * pe, gp_ref[...])


def _ffn(x, p, g_pre, w_up, conv_w, conv_b, w_down, g_ffn, w_ple, w_gate, g_ple, *, layer, tm, seq,
         col_chunk):
    n, d_model = x.shape
    d_ff = w_down.shape[0]
    row = lambda w: pl.BlockSpec((tm, w), lambda i: (i, 0))
    xp, xn = _halo_specs(tm, d_model, SUBLANES, n)
    const = lambda a: pl.BlockSpec(a.shape, lambda i: (0,) * a.ndim, pipeline_mode=pl.Buffered(1))
    return pl.pallas_call(
        functools.partial(_ffn_kernel, tm=tm, seq=seq, d_ff=d_ff, col_chunk=col_chunk),
        grid=(n // tm,),
        in_specs=[row(d_model), xp, xn, pl.BlockSpec((None, tm, p.shape[2]), lambda i: (layer, i, 0)),
                  const(g_pre), const(w_up), const(conv_w),
                  const(conv_b), const(w_down), const(g_ffn), const(w_ple), const(w_gate), const(g_ple)],
        out_specs=row(d_model),
        out_shape=jax.ShapeDtypeStruct((n, d_model), F32),
        scratch_shapes=[pltpu.VMEM((tm, d_ff), BF16)],
        compiler_params=_params(1),
        name="ffn",
    )(x, x, x, p, g_pre, w_up, conv_w, conv_b, w_down, g_ffn, w_ple, w_gate, g_ple)


def _tiles(seq):
    tm = min(MXU_WIDTH, seq)
    tm_in = min(2 * MXU_WIDTH, seq)
    tm_mix = min(4 * MXU_WIDTH, seq)
    tb = min(2 * MXU_WIDTH, seq)
    return tm, tm_in, tm_mix, tb


def _layer_weights(i, norm_mix_pre, norm_mix_post, norm_ffn_pre, norm_ffn_post, norm_ple_post, w_in,
                   conv_w, conv_b, w_branch_a, shift_mu, decay_w0, decay_w2, iclr_a0, iclr_a2, gate_g2,
                   k_k, k_a, r_k, gn_w, gn_b, w_branch_b, w_out, w_up, ffn_conv_w, ffn_conv_b, w_down,
                   w_ple, w_ple_gate):
    d_rwkv = k_k.shape[1]
    lora = decay_w2.shape[2]
    row = lambda a: a[i].reshape(1, -1)
    zeros = jnp.zeros((lora, d_rwkv), F32)
    lowrank = [jnp.concatenate([jnp.concatenate([decay_w2[i, d], zeros], axis=1),
                                jnp.concatenate([zeros, iclr_a2[i, d]], axis=1)], axis=0).astype(BF16)
               for d in range(2)]
    head_id = jnp.arange(d_rwkv) // HEAD
    return dict(
        g_mix_pre=row(norm_mix_pre), g_mix_post=row(norm_mix_post), g_ffn_pre=row(norm_ffn_pre),
        g_ffn_post=row(norm_ffn_post), g_ple_post=row(norm_ple_post),
        w_in=w_in[i].astype(BF16), conv_w=conv_w[i], conv_b=row(conv_b),
        w_a=w_branch_a[i].astype(BF16),
        mu=shift_mu[i][:, None, :], w0=decay_w0[i][:, None, :], a0=iclr_a0[i][:, None, :],
        lowrank=jnp.stack(lowrank), g2=gate_g2[i].astype(BF16), k_k=row(k_k), k_a=row(k_a),
        r_k=r_k[i].reshape(1, -1), gn_w=row(gn_w), gn_b=row(gn_b),
        bd=(head_id[:, None] == head_id[None, :]).astype(BF16),
        w_b=w_branch_b[i].astype(BF16), w_out=w_out[i].astype(BF16), w_up=w_up[i].astype(BF16),
        ffn_conv_w=ffn_conv_w[i], ffn_conv_b=row(ffn_conv_b), w_down=w_down[i].astype(BF16),
        w_ple=w_ple[i].astype(BF16), w_gate=w_ple_gate[i].astype(BF16))


def _layer(x, p, lw, masks, *, layer, bsz, seq):
    tm, tm_in, tm_mix, tb = _tiles(seq)
    d_conv = lw["conv_w"].shape[1]
    d_rwkv = lw["k_k"].shape[1]
    d_z = lw["mu"].shape[2]
    d_g = lw["g2"].shape[0]
    d_ff = lw["w_down"].shape[0]
    cghc, bgate, r, k, v, z, gd, sgc, sgr = _in_proj(
        x, lw["g_mix_pre"], lw["w_in"], tm=tm_in, d_conv=d_conv, d_rwkv=d_rwkv, d_z=2 * d_z, d_g=d_g)
    yf, yb, bonus = _wkv(r, k, v, z, lw["mu"], lw["w0"], lw["a0"], lw["lowrank"], lw["k_k"], lw["k_a"],
                         lw["r_k"], lw["bd"], masks, bsz=bsz, seq=seq, tb=tb)
    x = _mix_out(x, yf, yb, bonus, gd, cghc, bgate, sgc, sgr, lw["conv_w"], lw["conv_b"],
                 lw["w_a"], lw["g2"], lw["gn_w"], lw["gn_b"], lw["bd"], lw["w_b"],
                 lw["w_out"], lw["g_mix_post"], tm=tm_mix, seq=seq)
    return _ffn(x, p, lw["g_ffn_pre"], lw["w_up"], lw["ffn_conv_w"], lw["ffn_conv_b"], lw["w_down"],
                lw["g_ffn_post"], lw["w_ple"], lw["w_gate"], lw["g_ple_post"], layer=layer, tm=tm,
                seq=seq, col_chunk=math.gcd(d_ff, MXU_WIDTH))


def kernel(x_prompt, x_sample, p_prompt, p_sample, norm_mix_pre, norm_mix_post, norm_ffn_pre, norm_ffn_post, norm_ple_post, w_in, conv_w, conv_b, w_branch_a, shift_mu, decay_w0, decay_w2, iclr_a0, iclr_a2, gate_g2, k_k, k_a, r_k, gn_w, gn_b, w_branch_b, w_out, w_up, ffn_conv_w, ffn_conv_b, w_down, w_ple, w_ple_gate):
    weights = (norm_mix_pre, norm_mix_post, norm_ffn_pre, norm_ffn_post, norm_ple_post, w_in, conv_w,
               conv_b, w_branch_a, shift_mu, decay_w0, decay_w2, iclr_a0, iclr_a2, gate_g2, k_k, k_a,
               r_k, gn_w, gn_b, w_branch_b, w_out, w_up, ffn_conv_w, ffn_conv_b, w_down, w_ple,
               w_ple_gate)
    layers = [_layer_weights(i, *weights) for i in range(w_in.shape[0])]
    masks = _wkv_masks()
    outs = []
    for x, p in ((x_prompt, p_prompt), (x_sample, p_sample)):
        bsz, seq, d_model = x.shape
        y = x.reshape(bsz * seq, d_model)
        p = p.reshape(p.shape[0], bsz * seq, -1)
        for i, lw in enumerate(layers):
            y = _layer(y, p, lw, masks, layer=i, bsz=bsz, seq=seq)
        outs.append(y.reshape(bsz, seq, d_model))
    return tuple(outs)
```

```python
import functools
import math

import jax
import jax.numpy as jnp
from jax import lax
from jax.experimental import pallas as pl
from jax.experimental.pallas import tpu as pltpu

F32 = jnp.float32
BF16 = jnp.bfloat16

MXU_WIDTH = 256
SUBLANES = 8
BF16_ROWS = 16
VMEM_LIMIT_BYTES = 56 * 1024 * 1024

HEAD = 64
PAIR = 2 * HEAD
CHUNK = 64
CHUNKS_PER_STEP = 2
CONV_ROWS = 128
NORM_EPS = 1e-6
GN_EPS = HEAD * 1e-5
DECAY_SCALE = math.exp(-0.5)
GELU_C = math.sqrt(2.0 / math.pi)


def _sigmoid(x):
    return 1.0 / (1.0 + jnp.exp(-x))


def _rms(x, g):
    return x * lax.rsqrt(jnp.mean(x * x, axis=-1, keepdims=True) + NORM_EPS) * g


def _dot(a, b):
    return jnp.dot(a, b, preferred_element_type=F32)


def _dot_nt(a, b):
    return lax.dot_general(a, b, (((1,), (1,)), ((), ())), preferred_element_type=F32)


def _dot_tn(a, b):
    return lax.dot_general(a, b, (((0,), (0,)), ((), ())), preferred_element_type=F32)


def _split2(x):
    hi = x.astype(BF16)
    lo = (x - hi.astype(F32)).astype(BF16)
    return hi, lo


def _head_sum(x, bd):
    return _dot(x.astype(BF16), bd)


def _params(n_axes):
    return pltpu.CompilerParams(dimension_semantics=("arbitrary",) * n_axes,
                                vmem_limit_bytes=VMEM_LIMIT_BYTES)


def _full(shape):
    nd = len(shape)
    return pl.BlockSpec(shape, lambda *_: (0,) * nd)


def _in_proj_kernel(x_ref, g_ref, w_ref, cghc_ref, bg_ref, r_ref, k_ref, v_ref, z_ref, gd_ref,
                    sgc_ref, sgr_ref, *, d_conv, d_rwkv, d_z, d_g, d_model):
    u = _rms(x_ref[...], g_ref[...]).astype(BF16)
    o = 0
    hbc = _dot(u, w_ref[:, o:o + 3 * d_conv])
    cghc_ref[...] = (hbc[:, 2 * d_conv:] * hbc[:, :d_conv]).astype(BF16)
    bg_ref[...] = hbc[:, d_conv:2 * d_conv].astype(BF16)
    o += 3 * d_conv
    for ref in (r_ref, k_ref, v_ref):
        ref[...] = _dot(u, w_ref[:, o:o + d_rwkv])
        o += d_rwkv
    z_ref[...] = _dot(u, w_ref[:, o:o + d_z])
    o += d_z
    gd_ref[...] = _dot(u, w_ref[:, o:o + d_g])
    o += d_g
    for ref in (sgc_ref, sgr_ref):
        ref[...] = _sigmoid(_dot(u, w_ref[:, o:o + d_model])).astype(BF16)
        o += d_model


def _in_proj(x, g, w_in, *, tm, d_conv, d_rwkv, d_z, d_g):
    n, d_model = x.shape
    cols = w_in.shape[1]
    row = lambda w: pl.BlockSpec((tm, w), lambda i: (i, 0))
    widths = (d_conv, d_conv, d_rwkv, d_rwkv, d_rwkv, d_z, d_g, d_model, d_model)
    dtypes = (BF16, BF16, F32, F32, F32, F32, F32, BF16, BF16)
    return pl.pallas_call(
        functools.partial(_in_proj_kernel, d_conv=d_conv, d_rwkv=d_rwkv, d_z=d_z, d_g=d_g,
                          d_model=d_model),
        grid=(n // tm,),
        in_specs=[row(d_model), _full((1, d_model)), _full((d_model, cols))],
        out_specs=[row(w) for w in widths],
        out_shape=[jax.ShapeDtypeStruct((n, w), dt) for w, dt in zip(widths, dtypes)],
        compiler_params=_params(1),
        name="in_proj",
    )(x, g, w_in)


def _wkv_masks():
    ti = lax.broadcasted_iota(jnp.int32, (CHUNK, PAIR), 0)
    lane = lax.broadcasted_iota(jnp.int32, (CHUNK, PAIR), 1)
    si = lane & (HEAD - 1)
    li = lax.broadcasted_iota(jnp.int32, (CHUNK, CHUNK), 0)
    lj = lax.broadcasted_iota(jnp.int32, (CHUNK, CHUNK), 1)
    head_lo = (lane < HEAD).astype(BF16)
    cum, aa, lvl0, lvls = [], [], [], []
    for reverse in (False, True):
        strict = ((si > ti) if reverse else (si < ti)).astype(BF16)
        incl = ((si >= ti) if reverse else (si <= ti)).astype(BF16)
        cum.append(((lj >= li) if reverse else (lj <= li)).astype(BF16))
        aa.append(jnp.concatenate([jnp.concatenate([strict, strict], axis=1),
                                   jnp.concatenate([incl, incl], axis=1)], axis=0))
        levels = []
        sz = 1
        while sz < CHUNK:
            same_blk = (ti & -(2 * sz)) == (si & -(2 * sz))
            t_hi, s_hi = (ti & sz) != 0, (si & sz) != 0
            levels.append((same_blk & ((~t_hi & s_hi) if reverse else (t_hi & ~s_hi))).astype(BF16))
            sz *= 2
        lvl0.append(levels[0])
        lvls.append(jnp.stack([jnp.concatenate([m * head_lo, m * (1 - head_lo)], axis=0)
                               for m in levels[1:]]))
    bi = lax.broadcasted_iota(jnp.int32, (PAIR, PAIR), 0)
    bj = lax.broadcasted_iota(jnp.int32, (PAIR, PAIR), 1)
    return dict(cum=jnp.stack(cum), aa=jnp.stack(aa), lvl0=jnp.stack(lvl0), lvls=jnp.stack(lvls),
                eye=(si == ti).astype(BF16), heads=jnp.stack([head_lo, 1 - head_lo]),
                same_head=((bi < HEAD) == (bj < HEAD)).astype(F32))


def _wkv_features(z_ref, mu, w2, zc_s, *, reverse, tb):
    z = z_ref[...]
    rows = lax.broadcasted_iota(jnp.int32, z.shape, 0)
    carry = jnp.broadcast_to(zc_s[0:1, :], z.shape)
    if reverse:
        zs = jnp.where(rows == tb - 1, carry, pltpu.roll(z, tb - 1, 0))
        zc_s[...] = jnp.broadcast_to(z[0:1, :], zc_s.shape)
    else:
        zs = jnp.where(rows == 0, carry, pltpu.roll(z, 1, 0))
        zc_s[...] = jnp.broadcast_to(z[tb - 1:tb, :], zc_s.shape)
    zm = z + mu * (zs - z)
    lanes = lax.broadcasted_iota(jnp.int32, z.shape, 1)
    feat = jnp.where(lanes < HEAD, jnp.tanh(zm), zm).astype(BF16)
    return _dot(feat, w2)


def _wkv_prep(lo, k, w0, a0, k_k, k_a, bd):
    d = k.shape[1]
    lw = -DECAY_SCALE * _sigmoid(w0 + lo[:, :d])
    a = _sigmoid(a0 + lo[:, d:])
    kkr = k * k_k
    kkn = kkr * lax.rsqrt(jnp.maximum(_head_sum(kkr * kkr, bd), 1e-24))
    return lw, kkn, kkn * a, k * (1.0 + (a - 1.0) * k_a)


def _wkv_kernel(rf_ref, kf_ref, vf_ref, zf_ref, rb_ref, kb_ref, vb_ref, zb_ref, mu_ref, w0_ref, a0_ref,
                w2_ref, kk_ref, ka_ref, rk_ref, bd_ref, cum_ref, aa_ref, lvl0_ref, lvls_ref, eye_ref,
                heads_ref, same_ref, yf_ref, yb_ref, bonus_ref, state_s, zc_s, *, tb, n_pairs):
    nch = tb // CHUNK
    r_refs, k_refs, v_refs, z_refs, y_refs = ((rf_ref, rb_ref), (kf_ref, kb_ref), (vf_ref, vb_ref),
                                              (zf_ref, zb_ref), (yf_ref, yb_ref))

    @pl.when(pl.program_id(1) == 0)
    def _():
        state_s[...] = jnp.zeros_like(state_s)
        zc_s[...] = jnp.zeros_like(zc_s)

    lo = [_wkv_features(z_refs[dr], mu_ref[dr], w2_ref[dr], zc_s.at[dr], reverse=bool(dr), tb=tb)
          for dr in range(2)]
    edge = (CHUNK - 1, 0)

    lane_lo = lax.broadcasted_iota(jnp.int32, (CHUNK, PAIR), 1) < HEAD

    def stack(x):
        if x.dtype == F32:
            return jnp.concatenate([jnp.where(lane_lo, x, 0.0), jnp.where(lane_lo, 0.0, x)],
                                   axis=0).astype(BF16)
        return jnp.concatenate([x * heads_ref[0], x * heads_ref[1]], axis=0)

    group = math.gcd(nch, CHUNKS_PER_STEP)
    span = group * CHUNK
    n_groups = nch // group
    units = [(dr, p, j) for j in range(group) for dr in range(2) for p in range(n_pairs)]
    halves = (units[:len(units) // 2], units[len(units) // 2:])
    n_levels = lvls_ref.shape[1] + 1
    state = [state_s[i] for i in range(2 * n_pairs)]

    def rows(g, dr, j):
        base = g * span if dr == 0 else tb - (g + 1) * span
        off = base + (j if dr == 0 else group - 1 - j) * CHUNK
        return slice(off, off + CHUNK)

    def head(g, out):
        prep = {}
        for dr in range(2):
            blk = slice(rows(g, dr, 0 if dr == 0 else group - 1).start,
                        rows(g, dr, group - 1 if dr == 0 else 0).stop)
            prep[dr] = _wkv_prep(lo[dr][blk], k_refs[dr][blk, :], w0_ref[dr], a0_ref[dr], kk_ref[...],
                                 ka_ref[...], bd_ref[...])
            if dr == 0:
                rk_sum = _head_sum(rf_ref[blk, :] * kf_ref[blk, :] * rk_ref[...], bd_ref[...])
                bonus_ref[blk, :] = (rk_sum * vf_ref[blk, :]).astype(BF16)
            yield

        def take(which, dr, p, j):
            off = (j if dr == 0 else group - 1 - j) * CHUNK
            return prep[dr][which][off:off + CHUNK, p * PAIR:(p + 1) * PAIR]

        def load(refs, dr, p, j):
            return refs[dr][rows(g, dr, j), p * PAIR:(p + 1) * PAIR]

        for key in ("cw", "rw", "kkw", "v", "a_ab", "a_rb", "a_kv", "tinv", "w_end", "bk_end"):
            out[key] = {}
        for part in halves:
            for un in part:
                dr, p, j = un
                h1, h2 = _split2(take(0, dr, p, j))
                cs = _dot(cum_ref[dr], jnp.concatenate([h1, h2], axis=1))
                out["cw"][un] = cs[:, :PAIR] + cs[:, PAIR:]
            yield
        for part in halves:
            for un in part:
                dr, p, j = un
                cw = out["cw"].pop(un)
                e_pos, e_neg = jnp.exp(cw), jnp.exp(-cw)
                rw = (load(r_refs, dr, p, j) * e_pos).astype(BF16)
                kkw = (take(1, dr, p, j) * jnp.exp(cw - take(0, dr, p, j))).astype(BF16)
                binv = take(2, dr, p, j) * e_neg
                kinv = take(3, dr, p, j) * e_neg
                w_end = e_pos[edge[dr]:edge[dr] + 1, :]
                out["rw"][un], out["kkw"][un], out["w_end"][un] = rw, kkw, w_end
                out["v"][un] = load(v_refs, dr, p, j).astype(BF16)
                out["bk_end"][un] = jnp.concatenate([binv * w_end, kinv * w_end], axis=0).astype(BF16)
                aa = (_dot_nt(jnp.concatenate([kkw, rw], axis=0),
                              jnp.concatenate([stack(binv), stack(kinv)], axis=0)).astype(BF16)
                      * aa_ref[dr])
                a_ab = aa[:CHUNK, :PAIR]
                out["a_ab"][un] = a_ab
                out["a_rb"][un] = aa[CHUNK:, :PAIR]
                out["a_kv"][un] = aa[:, PAIR:]
                out["tinv"][un] = eye_ref[...] - a_ab * lvl0_ref[dr]
            yield

    def tail(g, h):
        nonlocal state
        av, tt = {}, {}
        for part in halves:
            for un in part:
                av[un] = _dot(h["a_kv"][un], stack(h["v"][un]))
            yield
        for part in halves:
            for un in part:
                tt[un] = _dot(h["tinv"][un],
                              jnp.concatenate([stack(h["kkw"][un]), stack(av[un][:CHUNK])], axis=1))
            yield
        for jj in range(group):
            uns = [un for un in units if un[2] == jj]
            ps = [_dot_nt(jnp.concatenate([tt[un][:, :PAIR].astype(BF16), h["rw"][un]], axis=0),
                          s.astype(BF16)) for un, s in zip(uns, state)]
            yield
            u = [-(x[:CHUNK] + tt[un][:, PAIR:]) for un, x in zip(uns, ps)]
            y = [x[CHUNK:] + _dot(h["a_rb"][un], stack(w)) + av[un][CHUNK:]
                 for un, x, w in zip(uns, ps, u)]
            upd = [_dot_tn(jnp.concatenate([x.astype(BF16), h["v"][un]], axis=0), h["bk_end"][un])
                   for un, x in zip(uns, u)]
            yield
            for (dr, p, j), x in zip(uns, y):
                y_refs[dr][rows(g, dr, j), p * PAIR:(p + 1) * PAIR] = x.astype(BF16)
            state = [s * h["w_end"][un] + x * same_ref[...] for un, s, x in zip(uns, state, upd)]
            yield

    def level(h, lvl):
        sz = 1 << lvl
        m1, sel = {}, {}
        for un in units:
            dr = un[0]
            t = h["tinv"][un]
            if sz % BF16_ROWS == 0:
                sel[un] = [r0 for r0 in range(0, CHUNK, sz) if ((r0 & sz) != 0) != bool(dr)]
                t = jnp.concatenate([t[r0:r0 + sz] for r0 in sel[un]], axis=0)
            a2 = jnp.concatenate([h["a_ab"][un], h["a_ab"][un]], axis=0)
            m1[un] = _dot(t, a2 * lvls_ref[dr, lvl - 1])
        yield
        for un in units:
            t = h["tinv"][un]
            m2 = _dot(m1[un].astype(BF16), stack(t)).astype(BF16)
            if un in sel:
                blocks = [t[r0:r0 + sz] for r0 in range(0, CHUNK, sz)]
                for i, r0 in enumerate(sel[un]):
                    blocks[r0 // sz] = blocks[r0 // sz] - m2[i * sz:(i + 1) * sz]
                h["tinv"][un] = jnp.concatenate(blocks, axis=0)
            else:
                h["tinv"][un] = t - m2
        yield

    def advance(gens):
        for gen in gens:
            next(gen, None)

    def drain(gens):
        for gen in gens:
            for _ in gen:
                pass

    heads = [dict() for _ in range(n_groups)]
    drain([head(0, heads[0])])
    side = []
    for g in range(n_groups):
        h = heads[g]
        if g + 1 < n_groups:
            side.append(head(g + 1, heads[g + 1]))
        for lvl in range(1, n_levels):
            for _ in level(h, lvl):
                advance(side)
        drain(side)
        side = [tail(g, h)]
    drain(side)
    for i, s in enumerate(state):
        state_s[i] = s


def _wkv(r, k, v, z, mu, w0, a0, w2, k_k, k_a, r_k, bd, masks, *, bsz, seq, tb):
    n, d = r.shape
    nt = seq // tb
    n_pairs = d // PAIR
    fmap = lambda b, i: (b * nt + i, 0)
    bmap = lambda b, i: (b * nt + nt - 1 - i, 0)
    fblk, bblk = pl.BlockSpec((tb, d), fmap), pl.BlockSpec((tb, d), bmap)
    zf = pl.BlockSpec((tb, PAIR), fmap)
    zb = pl.BlockSpec((tb, PAIR), lambda b, i: (bmap(b, i)[0], 1))
    consts = [masks[key] for key in ("cum", "aa", "lvl0", "lvls", "eye", "heads", "same_head")]
    return pl.pallas_call(
        functools.partial(_wkv_kernel, tb=tb, n_pairs=n_pairs),
        grid=(bsz, nt),
        in_specs=[fblk, fblk, fblk, zf, bblk, bblk, bblk, zb,
                  _full(mu.shape), _full(w0.shape), _full(a0.shape), _full(w2.shape),
                  _full(k_k.shape), _full(k_a.shape), _full(r_k.shape), _full(bd.shape)]
        + [_full(m.shape) for m in consts],
        out_specs=[fblk, bblk, fblk],
        out_shape=[jax.ShapeDtypeStruct((n, d), BF16)] * 3,
        scratch_shapes=[pltpu.VMEM((2 * n_pairs, PAIR, PAIR), F32), pltpu.VMEM((2, SUBLANES, PAIR), F32)],
        compiler_params=_params(2),
        name="wkv",
    )(r, k, v, z, r, k, v, z, mu, w0, a0, w2, k_k, k_a, r_k, bd, *consts)


def _shift_matrix(rows):
    shape = (2 * rows, rows + 2 * BF16_ROWS)
    r = lax.broadcasted_iota(jnp.int32, shape, 0)
    c = lax.broadcasted_iota(jnp.int32, shape, 1)
    target = jnp.where(r < rows, r + (BF16_ROWS - 1), r - rows + (BF16_ROWS + 1))
    return (c == target).astype(BF16)


def _conv3(x_ref, prev_ref, next_ref, shift, w_ref, b_ref, cols, first, last, r0):
    m = x_ref.shape[0]
    rows = shift.shape[0] // 2
    lo, hi = r0 - BF16_ROWS, r0 + rows + BF16_ROWS
    pieces = [x_ref[max(lo, 0):min(hi, m), cols]]
    if lo < 0:
        halo = prev_ref[:, cols]
        pieces.insert(0, jnp.where(first, jnp.zeros_like(halo), halo))
    if hi > m:
        halo = next_ref[:, cols]
        pieces.append(jnp.where(last, jnp.zeros_like(halo), halo))
    window = pieces[0] if len(pieces) == 1 else jnp.concatenate(pieces, axis=0)
    sh = _dot(shift, window)
    x = x_ref[r0:r0 + rows, cols].astype(F32)
    return (sh[:rows] * w_ref[0:1, cols] + x * w_ref[1:2, cols] + sh[rows:] * w_ref[2:3, cols]
            + b_ref[:, cols])


def _halo_specs(tm, width, rows_per_blk, n_rows):
    nb = tm // rows_per_blk
    last_blk = n_rows // rows_per_blk - 1
    prev = pl.BlockSpec((rows_per_blk, width), lambda i: (jnp.maximum(i * nb - 1, 0), 0))
    nxt = pl.BlockSpec((rows_per_blk, width), lambda i: (jnp.minimum((i + 1) * nb, last_blk), 0))
    return prev, nxt


def _mix_out_kernel(x_ref, yf_ref, yb_ref, bonus_ref, gd_ref, c_ref, cp_ref, cn_ref, bg_ref,
                    sgc_ref, sgr_ref, cw_ref, cb_ref, wa_ref, g2_ref, gnw_ref, gnb_ref,
                    bd_ref, wb_ref, wo_ref, g_ref, o_ref, *, tm, seq):
    i = pl.program_id(0)
    first = (i * tm) % seq == 0
    last = ((i + 1) * tm) % seq == 0
    rows = min(CONV_ROWS, tm)
    shift = _shift_matrix(rows)
    conv = jnp.concatenate([_conv3(c_ref, cp_ref, cn_ref, shift, cw_ref, cb_ref, slice(None), first,
                                   last, r0) for r0 in range(0, tm, rows)], axis=0)
    y_conv = _dot((bg_ref[...].astype(F32) * conv).astype(BF16), wa_ref[...])
    bd = bd_ref[...]
    y = yf_ref[...].astype(F32) + yb_ref[...].astype(F32)
    mean = _head_sum(y, bd) * (1.0 / HEAD)
    yc = y - mean
    var = _head_sum(yc * yc, bd) * (1.0 / HEAD)
    yn = yc * lax.rsqrt(var + GN_EPS) * gnw_ref[...] + gnb_ref[...]
    gate = _dot(_sigmoid(gd_ref[...]).astype(BF16), g2_ref[...])
    y_rwkv = _dot(((yn + bonus_ref[...].astype(F32)) * gate).astype(BF16), wb_ref[...])
    merged = sgc_ref[...].astype(F32) * y_conv + sgr_ref[...].astype(F32) * y_rwkv
    m = _dot(merged.astype(BF16), wo_ref[...])
    o_ref[...] = x_ref[...] + _rms(m, g_ref[...])


def _mix_out(x, yf, yb, bonus, gd, cghc, bgate, sgc, sgr, conv_w, conv_b, w_a, g2, gn_w, gn_b,
             bd, w_b, w_out, g_post, *, tm, seq):
    n, d_model = x.shape
    d = yf.shape[1]
    d_conv = cghc.shape[1]
    row = lambda w: pl.BlockSpec((tm, w), lambda i: (i, 0))
    cp, cn = _halo_specs(tm, d_conv, BF16_ROWS, n)
    return pl.pallas_call(
        functools.partial(_mix_out_kernel, tm=tm, seq=seq),
        grid=(n // tm,),
        in_specs=[row(d_model), row(d), row(d), row(d), row(gd.shape[1]),
                  row(d_conv), cp, cn, row(d_conv), row(d_model), row(d_model),
                  _full(conv_w.shape), _full(conv_b.shape), _full(w_a.shape), _full(g2.shape),
                  _full(gn_w.shape), _full(gn_b.shape), _full(bd.shape),
                  _full(w_b.shape), _full(w_out.shape), _full(g_post.shape)],
        out_specs=row(d_model),
        out_shape=jax.ShapeDtypeStruct((n, d_model), F32),
        compiler_params=_params(1),
        name="mix_out",
    )(x, yf, yb, bonus, gd, cghc, cghc, cghc, bgate, sgc, sgr, conv_w, conv_b, w_a, g2, gn_w,
      gn_b, bd, w_b, w_out, g_post)


def _ffn_kernel(x_ref, xp_ref, xn_ref, p_ref, gpre_ref, wu_ref, cw_ref, cb_ref, wd_ref, gf_ref, wp_ref,
                wg_ref, gp_ref, o_ref, act_s, *, tm, seq, d_ff, col_chunk):
    i = pl.program_id(0)
    first = (i * tm) % seq == 0
    last = ((i + 1) * tm) % seq == 0
    g = gpre_ref[...]
    u_prev = jnp.where(first, 0.0, _rms(xp_ref[...], g))
    u_next = jnp.where(last, 0.0, _rms(xn_ref[...], g))
    u = jnp.concatenate([u_prev, _rms(x_ref[...], g), u_next], axis=0).astype(BF16)
    m = tm + 2 * SUBLANES
    for c0 in range(0, d_ff, col_chunk):
        conv = []
        for off in (c0, d_ff + c0):
            cols = slice(off, off + col_chunk)
            h = _dot(u, wu_ref[:, cols])
            conv.append(pltpu.roll(h, 1, 0) * cw_ref[0:1, cols] + h * cw_ref[1:2, cols]
                        + pltpu.roll(h, m - 1, 0) * cw_ref[2:3, cols] + cb_ref[:, cols])
        hg, hv = conv
        gelu = 0.5 * hg * (1.0 + jnp.tanh(GELU_C * (hg + 0.044715 * (hg * hg * hg))))
        act_s[:, c0:c0 + col_chunk] = (gelu * hv)[SUBLANES:SUBLANES + tm].astype(BF16)
    f = _dot(act_s[...], wd_ref[...])
    x = x_ref[...] + _rms(f, gf_ref[...])
    gate = _sigmoid(_dot(x.astype(BF16), wg_ref[...]))
    pe = _dot(p_ref[...].astype(BF16), wp_ref[...])
    o_ref[...] = x + _rms(gate * pe, gp_ref[...])


def _ffn(x, p, g_pre, w_up, conv_w, conv_b, w_down, g_ffn, w_ple, w_gate, g_ple, *, layer, tm, seq,
         col_chunk):
    n, d_model = x.shape
    d_ff = w_down.shape[0]
    row = lambda w: pl.BlockSpec((tm, w), lambda i: (i, 0))
    xp, xn = _halo_specs(tm, d_model, SUBLANES, n)
    const = lambda a: pl.BlockSpec(a.shape, lambda i: (0,) * a.ndim, pipeline_mode=pl.Buffered(1))
    return pl.pallas_call(
        functools.partial(_ffn_kernel, tm=tm, seq=seq, d_ff=d_ff, col_chunk=col_chunk),
        grid=(n // tm,),
        in_specs=[row(d_model), xp, xn, pl.BlockSpec((None, tm, p.shape[2]), lambda i: (layer, i, 0)),
                  const(g_pre), const(w_up), const(conv_w),
                  const(conv_b), const(w_down), const(g_ffn), const(w_ple), const(w_gate), const(g_ple)],
        out_specs=row(d_model),
        out_shape=jax.ShapeDtypeStruct((n, d_model), F32),
        scratch_shapes=[pltpu.VMEM((tm, d_ff), BF16)],
        compiler_params=_params(1),
        name="ffn",
    )(x, x, x, p, g_pre, w_up, conv_w, conv_b, w_down, g_ffn, w_ple, w_gate, g_ple)


def _tiles(seq):
    tm = min(MXU_WIDTH, seq)
    tm_in = min(2 * MXU_WIDTH, seq)
    tm_mix = min(4 * MXU_WIDTH, seq)
    tb = min(2 * MXU_WIDTH, seq)
    assert all(seq % t == 0 and t % BF16_ROWS == 0 for t in (tm, tm_in, tm_mix, tb)), seq
    assert tb % CHUNK == 0, tb
    return tm, tm_in, tm_mix, tb


def _layer_weights(i, norm_mix_pre, norm_mix_post, norm_ffn_pre, norm_ffn_post, norm_ple_post, w_in,
                   conv_w, conv_b, w_branch_a, shift_mu, decay_w0, decay_w2, iclr_a0, iclr_a2, gate_g2,
                   k_k, k_a, r_k, gn_w, gn_b, w_branch_b, w_out, w_up, ffn_conv_w, ffn_conv_b, w_down,
                   w_ple, w_ple_gate):
    d_rwkv = k_k.shape[1]
    lora = decay_w2.shape[2]
    row = lambda a: a[i].reshape(1, -1)
    zeros = jnp.zeros((lora, d_rwkv), F32)
    lowrank = [jnp.concatenate([jnp.concatenate([decay_w2[i, d], zeros], axis=1),
                                jnp.concatenate([zeros, iclr_a2[i, d]], axis=1)], axis=0).astype(BF16)
               for d in range(2)]
    head_id = jnp.arange(d_rwkv) // HEAD
    return dict(
        g_mix_pre=row(norm_mix_pre), g_mix_post=row(norm_mix_post), g_ffn_pre=row(norm_ffn_pre),
        g_ffn_post=row(norm_ffn_post), g_ple_post=row(norm_ple_post),
        w_in=w_in[i].astype(BF16), conv_w=conv_w[i], conv_b=row(conv_b),
        w_a=w_branch_a[i].astype(BF16),
        mu=shift_mu[i][:, None, :], w0=decay_w0[i][:, None, :], a0=iclr_a0[i][:, None, :],
        lowrank=jnp.stack(lowrank), g2=gate_g2[i].astype(BF16), k_k=row(k_k), k_a=row(k_a),
        r_k=r_k[i].reshape(1, -1), gn_w=row(gn_w), gn_b=row(gn_b),
        bd=(head_id[:, None] == head_id[None, :]).astype(BF16),
        w_b=w_branch_b[i].astype(BF16), w_out=w_out[i].astype(BF16), w_up=w_up[i].astype(BF16),
        ffn_conv_w=ffn_conv_w[i], ffn_conv_b=row(ffn_conv_b), w_down=w_down[i].astype(BF16),
        w_ple=w_ple[i].astype(BF16), w_gate=w_ple_gate[i].astype(BF16))


def _layer(x, p, lw, masks, *, layer, bsz, seq):
    tm, tm_in, tm_mix, tb = _tiles(seq)
    d_conv = lw["conv_w"].shape[1]
    d_rwkv = lw["k_k"].shape[1]
    d_z = lw["mu"].shape[2]
    d_g = lw["g2"].shape[0]
    d_ff = lw["w_down"].shape[0]
    cghc, bgate, r, k, v, z, gd, sgc, sgr = _in_proj(
        x, lw["g_mix_pre"], lw["w_in"], tm=tm_in, d_conv=d_conv, d_rwkv=d_rwkv, d_z=2 * d_z, d_g=d_g)
    yf, yb, bonus = _wkv(r, k, v, z, lw["mu"], lw["w0"], lw["a0"], lw["lowrank"], lw["k_k"], lw["k_a"],
                         lw["r_k"], lw["bd"], masks, bsz=bsz, seq=seq, tb=tb)
    x = _mix_out(x, yf, yb, bonus, gd, cghc, bgate, sgc, sgr, lw["conv_w"], lw["conv_b"],
                 lw["w_a"], lw["g2"], lw["gn_w"], lw["gn_b"], lw["bd"], lw["w_b"],
                 lw["w_out"], lw["g_mix_post"], tm=tm_mix, seq=seq)
    return _ffn(x, p, lw["g_ffn_pre"], lw["w_up"], lw["ffn_conv_w"], lw["ffn_conv_b"], lw["w_down"],
                lw["g_ffn_post"], lw["w_ple"], lw["w_gate"], lw["g_ple_post"], layer=layer, tm=tm,
                seq=seq, col_chunk=math.gcd(d_ff, MXU_WIDTH))


def kernel(x_prompt, x_sample, p_prompt, p_sample, norm_mix_pre, norm_mix_post, norm_ffn_pre, norm_ffn_post, norm_ple_post, w_in, conv_w, conv_b, w_branch_a, shift_mu, decay_w0, decay_w2, iclr_a0, iclr_a2, gate_g2, k_k, k_a, r_k, gn_w, gn_b, w_branch_b, w_out, w_up, ffn_conv_w, ffn_conv_b, w_down, w_ple, w_ple_gate):
    weights = (norm_mix_pre, norm_mix_post, norm_ffn_pre, norm_ffn_post, norm_ple_post, w_in, conv_w,
               conv_b, w_branch_a, shift_mu, decay_w0, decay_w2, iclr_a0, iclr_a2, gate_g2, k_k, k_a,
               r_k, gn_w, gn_b, w_branch_b, w_out, w_up, ffn_conv_w, ffn_conv_b, w_down, w_ple,
               w_ple_gate)
    layers = [_layer_weights(i, *weights) for i in range(w_in.shape[0])]
    masks = _wkv_masks()
    outs = []
    for x, p in ((x_prompt, p_prompt), (x_sample, p_sample)):
        bsz, seq, d_model = x.shape
        y = x.reshape(bsz * seq, d_model)
        p = p.reshape(p.shape[0], bsz * seq, -1)
        for i, lw in enumerate(layers):
            y = _layer(y, p, lw, masks, layer=i, bsz=bsz, seq=seq)
        outs.append(y.reshape(bsz, seq, d_model))
    return tuple(outs)
```

```python
import functools
import math

import jax
import jax.numpy as jnp
from jax import lax
from jax.experimental import pallas as pl
from jax.experimental.pallas import tpu as pltpu

F32 = jnp.float32
BF16 = jnp.bfloat16

MXU_WIDTH = 256
SUBLANES = 8
BF16_ROWS = 16
VMEM_LIMIT_BYTES = 56 * 1024 * 1024

HEAD = 64
PAIR = 2 * HEAD
CHUNK = 64
CHUNKS_PER_STEP = 2
NORM_EPS = 1e-6
GN_EPS = HEAD * 1e-5
DECAY_SCALE = math.exp(-0.5)
GELU_C = math.sqrt(2.0 / math.pi)


def _sigmoid(x):
    return 1.0 / (1.0 + jnp.exp(-x))


def _rms(x, g):
    return x * lax.rsqrt(jnp.mean(x * x, axis=-1, keepdims=True) + NORM_EPS) * g


def _dot(a, b):
    return jnp.dot(a, b, preferred_element_type=F32)


def _dot_nt(a, b):
    return lax.dot_general(a, b, (((1,), (1,)), ((), ())), preferred_element_type=F32)


def _dot_tn(a, b):
    return lax.dot_general(a, b, (((0,), (0,)), ((), ())), preferred_element_type=F32)


def _split2(x):
    hi = x.astype(BF16)
    lo = (x - hi.astype(F32)).astype(BF16)
    return hi, lo


def _head_sum(x, bd):
    return _dot(x.astype(BF16), bd)


def _params(n_axes):
    return pltpu.CompilerParams(dimension_semantics=("arbitrary",) * n_axes,
                                vmem_limit_bytes=VMEM_LIMIT_BYTES)


def _full(shape):
    nd = len(shape)
    return pl.BlockSpec(shape, lambda *_: (0,) * nd)


def _in_proj_kernel(x_ref, g_ref, w_ref, cghc_ref, bg_ref, r_ref, k_ref, v_ref, z_ref, gd_ref,
                    sgc_ref, sgr_ref, *, d_conv, d_rwkv, d_z, d_g, d_model):
    u = _rms(x_ref[...], g_ref[...]).astype(BF16)
    o = 0
    hbc = _dot(u, w_ref[:, o:o + 3 * d_conv])
    cghc_ref[...] = (hbc[:, 2 * d_conv:] * hbc[:, :d_conv]).astype(BF16)
    bg_ref[...] = hbc[:, d_conv:2 * d_conv].astype(BF16)
    o += 3 * d_conv
    for ref in (r_ref, k_ref, v_ref):
        ref[...] = _dot(u, w_ref[:, o:o + d_rwkv])
        o += d_rwkv
    z_ref[...] = _dot(u, w_ref[:, o:o + d_z])
    o += d_z
    gd_ref[...] = _dot(u, w_ref[:, o:o + d_g])
    o += d_g
    for ref in (sgc_ref, sgr_ref):
        ref[...] = _sigmoid(_dot(u, w_ref[:, o:o + d_model])).astype(BF16)
        o += d_model


def _in_proj(x, g, w_in, *, tm, d_conv, d_rwkv, d_z, d_g):
    n, d_model = x.shape
    cols = w_in.shape[1]
    row = lambda w: pl.BlockSpec((tm, w), lambda i: (i, 0))
    widths = (d_conv, d_conv, d_rwkv, d_rwkv, d_rwkv, d_z, d_g, d_model, d_model)
    dtypes = (BF16, BF16, F32, F32, F32, F32, F32, BF16, BF16)
    return pl.pallas_call(
        functools.partial(_in_proj_kernel, d_conv=d_conv, d_rwkv=d_rwkv, d_z=d_z, d_g=d_g,
                          d_model=d_model),
        grid=(n // tm,),
        in_specs=[row(d_model), _full((1, d_model)), _full((d_model, cols))],
        out_specs=[row(w) for w in widths],
        out_shape=[jax.ShapeDtypeStruct((n, w), dt) for w, dt in zip(widths, dtypes)],
        compiler_params=_params(1),
        name="in_proj",
    )(x, g, w_in)


def _wkv_masks():
    ti = lax.broadcasted_iota(jnp.int32, (CHUNK, PAIR), 0)
    lane = lax.broadcasted_iota(jnp.int32, (CHUNK, PAIR), 1)
    si = lane & (HEAD - 1)
    li = lax.broadcasted_iota(jnp.int32, (CHUNK, CHUNK), 0)
    lj = lax.broadcasted_iota(jnp.int32, (CHUNK, CHUNK), 1)
    head_lo = (lane < HEAD).astype(BF16)
    cum, aa, lvl0, lvls = [], [], [], []
    for reverse in (False, True):
        strict = ((si > ti) if reverse else (si < ti)).astype(BF16)
        incl = ((si >= ti) if reverse else (si <= ti)).astype(BF16)
        cum.append(((lj >= li) if reverse else (lj <= li)).astype(BF16))
        aa.append(jnp.concatenate([jnp.concatenate([strict, strict], axis=1),
                                   jnp.concatenate([incl, incl], axis=1)], axis=0))
        levels = []
        sz = 1
        while sz < CHUNK:
            same_blk = (ti & -(2 * sz)) == (si & -(2 * sz))
            t_hi, s_hi = (ti & sz) != 0, (si & sz) != 0
            levels.append((same_blk & ((~t_hi & s_hi) if reverse else (t_hi & ~s_hi))).astype(BF16))
            sz *= 2
        lvl0.append(levels[0])
        lvls.append(jnp.stack([jnp.concatenate([m * head_lo, m * (1 - head_lo)], axis=0)
                               for m in levels[1:]]))
    bi = lax.broadcasted_iota(jnp.int32, (PAIR, PAIR), 0)
    bj = lax.broadcasted_iota(jnp.int32, (PAIR, PAIR), 1)
    return dict(cum=jnp.stack(cum), aa=jnp.stack(aa), lvl0=jnp.stack(lvl0), lvls=jnp.stack(lvls),
                eye=(si == ti).astype(BF16), heads=jnp.stack([head_lo, 1 - head_lo]),
                same_head=((bi < HEAD) == (bj < HEAD)).astype(F32))


def _wkv_features(z_ref, mu, w2, zc_s, *, reverse, tb):
    z = z_ref[...]
    rows = lax.broadcasted_iota(jnp.int32, z.shape, 0)
    carry = jnp.broadcast_to(zc_s[0:1, :], z.shape)
    if reverse:
        zs = jnp.where(rows == tb - 1, carry, pltpu.roll(z, tb - 1, 0))
        zc_s[...] = jnp.broadcast_to(z[0:1, :], zc_s.shape)
    else:
        zs = jnp.where(rows == 0, carry, pltpu.roll(z, 1, 0))
        zc_s[...] = jnp.broadcast_to(z[tb - 1:tb, :], zc_s.shape)
    zm = z + mu * (zs - z)
    lanes = lax.broadcasted_iota(jnp.int32, z.shape, 1)
    feat = jnp.where(lanes < HEAD, jnp.tanh(zm), zm).astype(BF16)
    return _dot(feat, w2)


def _wkv_prep(lo, k, w0, a0, k_k, k_a, bd):
    d = k.shape[1]
    lw = -DECAY_SCALE * _sigmoid(w0 + lo[:, :d])
    a = _sigmoid(a0 + lo[:, d:])
    kkr = k * k_k
    kkn = kkr * lax.rsqrt(jnp.maximum(_head_sum(kkr * kkr, bd), 1e-24))
    return lw, kkn, kkn * a, k * (1.0 + (a - 1.0) * k_a)


def _wkv_kernel(rf_ref, kf_ref, vf_ref, zf_ref, rb_ref, kb_ref, vb_ref, zb_ref, mu_ref, w0_ref, a0_ref,
                w2_ref, kk_ref, ka_ref, rk_ref, bd_ref, cum_ref, aa_ref, lvl0_ref, lvls_ref, eye_ref,
                heads_ref, same_ref, yf_ref, yb_ref, bonus_ref, state_s, zc_s, *, tb, n_pairs):
    nch = tb // CHUNK
    r_refs, k_refs, v_refs, z_refs, y_refs = ((rf_ref, rb_ref), (kf_ref, kb_ref), (vf_ref, vb_ref),
                                              (zf_ref, zb_ref), (yf_ref, yb_ref))

    @pl.when(pl.program_id(1) == 0)
    def _():
        state_s[...] = jnp.zeros_like(state_s)
        zc_s[...] = jnp.zeros_like(zc_s)

    lo = [_wkv_features(z_refs[dr], mu_ref[dr], w2_ref[dr], zc_s.at[dr], reverse=bool(dr), tb=tb)
          for dr in range(2)]
    edge = (CHUNK - 1, 0)

    lane_lo = lax.broadcasted_iota(jnp.int32, (CHUNK, PAIR), 1) < HEAD

    def stack(x):
        if x.dtype == F32:
            return jnp.concatenate([jnp.where(lane_lo, x, 0.0), jnp.where(lane_lo, 0.0, x)],
                                   axis=0).astype(BF16)
        return jnp.concatenate([x * heads_ref[0], x * heads_ref[1]], axis=0)

    group = math.gcd(nch, CHUNKS_PER_STEP)
    span = group * CHUNK
    n_groups = nch // group
    units = [(dr, p, j) for j in range(group) for dr in range(2) for p in range(n_pairs)]
    halves = (units[:len(units) // 2], units[len(units) // 2:])
    n_levels = lvls_ref.shape[1] + 1
    state = [state_s[i] for i in range(2 * n_pairs)]

    def rows(g, dr, j):
        base = g * span if dr == 0 else tb - (g + 1) * span
        off = base + (j if dr == 0 else group - 1 - j) * CHUNK
        return slice(off, off + CHUNK)

    def head(g, out):
        prep = {}
        for dr in range(2):
            blk = slice(rows(g, dr, 0 if dr == 0 else group - 1).start,
                        rows(g, dr, group - 1 if dr == 0 else 0).stop)
            prep[dr] = _wkv_prep(lo[dr][blk], k_refs[dr][blk, :], w0_ref[dr], a0_ref[dr], kk_ref[...],
                                 ka_ref[...], bd_ref[...])
            if dr == 0:
                rk_sum = _head_sum(rf_ref[blk, :] * kf_ref[blk, :] * rk_ref[...], bd_ref[...])
                bonus_ref[blk, :] = (rk_sum * vf_ref[blk, :]).astype(BF16)
            yield

        def take(which, dr, p, j):
            off = (j if dr == 0 else group - 1 - j) * CHUNK
            return prep[dr][which][off:off + CHUNK, p * PAIR:(p + 1) * PAIR]

        def load(refs, dr, p, j):
            return refs[dr][rows(g, dr, j), p * PAIR:(p + 1) * PAIR]

        for key in ("cw", "rw", "kkw", "v", "a_ab", "a_rb", "a_kv", "tinv", "w_end", "bk_end"):
            out[key] = {}
        for part in halves:
            for un in part:
                dr, p, j = un
                h1, h2 = _split2(take(0, dr, p, j))
                cs = _dot(cum_ref[dr], jnp.concatenate([h1, h2], axis=1))
                out["cw"][un] = cs[:, :PAIR] + cs[:, PAIR:]
            yield
        for part in halves:
            for un in part:
                dr, p, j = un
                cw = out["cw"].pop(un)
                e_pos, e_neg = jnp.exp(cw), jnp.exp(-cw)
                rw = (load(r_refs, dr, p, j) * e_pos).astype(BF16)
                kkw = (take(1, dr, p, j) * jnp.exp(cw - take(0, dr, p, j))).astype(BF16)
                binv = take(2, dr, p, j) * e_neg
                kinv = take(3, dr, p, j) * e_neg
                w_end = e_pos[edge[dr]:edge[dr] + 1, :]
                out["rw"][un], out["kkw"][un], out["w_end"][un] = rw, kkw, w_end
                out["v"][un] = load(v_refs, dr, p, j).astype(BF16)
                out["bk_end"][un] = jnp.concatenate([binv * w_end, kinv * w_end], axis=0).astype(BF16)
                aa = (_dot_nt(jnp.concatenate([kkw, rw], axis=0),
                              jnp.concatenate([stack(binv), stack(kinv)], axis=0)).astype(BF16)
                      * aa_ref[dr])
                a_ab = aa[:CHUNK, :PAIR]
                out["a_ab"][un] = a_ab
                out["a_rb"][un] = aa[CHUNK:, :PAIR]
                out["a_kv"][un] = aa[:, PAIR:]
                out["tinv"][un] = eye_ref[...] - a_ab * lvl0_ref[dr]
            yield

    def tail(g, h):
        nonlocal state
        av, tt = {}, {}
        for part in halves:
            for un in part:
                av[un] = _dot(h["a_kv"][un], stack(h["v"][un]))
            yield
        for part in halves:
            for un in part:
                tt[un] = _dot(h["tinv"][un],
                              jnp.concatenate([stack(h["kkw"][un]), stack(av[un][:CHUNK])], axis=1))
            yield
        for jj in range(group):
            uns = [un for un in units if un[2] == jj]
            ps = [_dot_nt(jnp.concatenate([tt[un][:, :PAIR].astype(BF16), h["rw"][un]], axis=0),
                          s.astype(BF16)) for un, s in zip(uns, state)]
            yield
            u = [-(x[:CHUNK] + tt[un][:, PAIR:]) for un, x in zip(uns, ps)]
            y = [x[CHUNK:] + _dot(h["a_rb"][un], stack(w)) + av[un][CHUNK:]
                 for un, x, w in zip(uns, ps, u)]
            upd = [_dot_tn(jnp.concatenate([x.astype(BF16), h["v"][un]], axis=0), h["bk_end"][un])
                   for un, x in zip(uns, u)]
            yield
            for (dr, p, j), x in zip(uns, y):
                y_refs[dr][rows(g, dr, j), p * PAIR:(p + 1) * PAIR] = x.astype(BF16)
            state = [s * h["w_end"][un] + x * same_ref[...] for un, s, x in zip(uns, state, upd)]
            yield

    def level(h, lvl):
        sz = 1 << lvl
        m1, sel = {}, {}
        for un in units:
            dr = un[0]
            t = h["tinv"][un]
            if sz % BF16_ROWS == 0:
                sel[un] = [r0 for r0 in range(0, CHUNK, sz) if ((r0 & sz) != 0) != bool(dr)]
                t = jnp.concatenate([t[r0:r0 + sz] for r0 in sel[un]], axis=0)
            a2 = jnp.concatenate([h["a_ab"][un], h["a_ab"][un]], axis=0)
            m1[un] = _dot(t, a2 * lvls_ref[dr, lvl - 1])
        yield
        for un in units:
            t = h["tinv"][un]
            m2 = _dot(m1[un].astype(BF16), stack(t)).astype(BF16)
            if un in sel:
                blocks = [t[r0:r0 + sz] for r0 in range(0, CHUNK, sz)]
                for i, r0 in enumerate(sel[un]):
                    blocks[r0 // sz] = blocks[r0 // sz] - m2[i * sz:(i + 1) * sz]
                h["tinv"][un] = jnp.concatenate(blocks, axis=0)
            else:
                h["tinv"][un] = t - m2
        yield

    def advance(gens):
        for gen in gens:
            next(gen, None)

    def drain(gens):
        for gen in gens:
            for _ in gen:
                pass

    heads = [dict() for _ in range(n_groups)]
    drain([head(0, heads[0])])
    side = []
    for g in range(n_groups):
        h = heads[g]
        if g + 1 < n_groups:
            side.append(head(g + 1, heads[g + 1]))
        for lvl in range(1, n_levels):
            for _ in level(h, lvl):
                advance(side)
        drain(side)
        side = [tail(g, h)]
    drain(side)
    for i, s in enumerate(state):
        state_s[i] = s


def _wkv(r, k, v, z, mu, w0, a0, w2, k_k, k_a, r_k, bd, masks, *, bsz, seq, tb):
    n, d = r.shape
    nt = seq // tb
    n_pairs = d // PAIR
    fmap = lambda b, i: (b * nt + i, 0)
    bmap = lambda b, i: (b * nt + nt - 1 - i, 0)
    fblk, bblk = pl.BlockSpec((tb, d), fmap), pl.BlockSpec((tb, d), bmap)
    zf = pl.BlockSpec((tb, PAIR), fmap)
    zb = pl.BlockSpec((tb, PAIR), lambda b, i: (bmap(b, i)[0], 1))
    consts = [masks[key] for key in ("cum", "aa", "lvl0", "lvls", "eye", "heads", "same_head")]
    return pl.pallas_call(
        functools.partial(_wkv_kernel, tb=tb, n_pairs=n_pairs),
        grid=(bsz, nt),
        in_specs=[fblk, fblk, fblk, zf, bblk, bblk, bblk, zb,
                  _full(mu.shape), _full(w0.shape), _full(a0.shape), _full(w2.shape),
                  _full(k_k.shape), _full(k_a.shape), _full(r_k.shape), _full(bd.shape)]
        + [_full(m.shape) for m in consts],
        out_specs=[fblk, bblk, fblk],
        out_shape=[jax.ShapeDtypeStruct((n, d), BF16)] * 3,
        scratch_shapes=[pltpu.VMEM((2 * n_pairs, PAIR, PAIR), F32), pltpu.VMEM((2, SUBLANES, PAIR), F32)],
        compiler_params=_params(2),
        name="wkv",
    )(r, k, v, z, r, k, v, z, mu, w0, a0, w2, k_k, k_a, r_k, bd, *consts)


def _halo_specs(tm, width, rows_per_blk, n_rows):
    nb = tm // rows_per_blk
    last_blk = n_rows // rows_per_blk - 1
    prev = pl.BlockSpec((rows_per_blk, width), lambda i: (jnp.maximum(i * nb - 1, 0), 0))
    nxt = pl.BlockSpec((rows_per_blk, width), lambda i: (jnp.minimum((i + 1) * nb, last_blk), 0))
    return prev, nxt


def _mix_out_kernel(x_ref, yf_ref, yb_ref, bonus_ref, gd_ref, c_ref, cp_ref, cn_ref, bg_ref,
                    sgc_ref, sgr_ref, cw_ref, cb_ref, wa_ref, g2_ref, gnw_ref, gnb_ref,
                    bd_ref, wb_ref, wo_ref, g_ref, o_ref, *, tm, seq):
    i = pl.program_id(0)
    first = (i * tm) % seq == 0
    last = ((i + 1) * tm) % seq == 0
    halo_p, halo_n = cp_ref[...], cn_ref[...]
    c = jnp.concatenate([jnp.where(first, jnp.zeros_like(halo_p), halo_p), c_ref[...],
                         jnp.where(last, jnp.zeros_like(halo_n), halo_n)], axis=0).astype(F32)
    m = tm + 2 * BF16_ROWS
    conv = (pltpu.roll(c, 1, 0) * cw_ref[0:1, :] + c * cw_ref[1:2, :]
            + pltpu.roll(c, m - 1, 0) * cw_ref[2:3, :] + cb_ref[...])[BF16_ROWS:BF16_ROWS + tm]
    y_conv = _dot((bg_ref[...].astype(F32) * conv).astype(BF16), wa_ref[...])
    bd = bd_ref[...]
    y = yf_ref[...].astype(F32) + yb_ref[...].astype(F32)
    mean = _head_sum(y, bd) * (1.0 / HEAD)
    yc = y - mean
    var = _head_sum(yc * yc, bd) * (1.0 / HEAD)
    yn = yc * lax.rsqrt(var + GN_EPS) * gnw_ref[...] + gnb_ref[...]
    gate = _dot(_sigmoid(gd_ref[...]).astype(BF16), g2_ref[...])
    y_rwkv = _dot(((yn + bonus_ref[...].astype(F32)) * gate).astype(BF16), wb_ref[...])
    merged = sgc_ref[...].astype(F32) * y_conv + sgr_ref[...].astype(F32) * y_rwkv
    m = _dot(merged.astype(BF16), wo_ref[...])
    o_ref[...] = x_ref[...] + _rms(m, g_ref[...])


def _mix_out(x, yf, yb, bonus, gd, cghc, bgate, sgc, sgr, conv_w, conv_b, w_a, g2, gn_w, gn_b,
             bd, w_b, w_out, g_post, *, tm, seq):
    n, d_model = x.shape
    d = yf.shape[1]
    d_conv = cghc.shape[1]
    row = lambda w: pl.BlockSpec((tm, w), lambda i: (i, 0))
    cp, cn = _halo_specs(tm, d_conv, BF16_ROWS, n)
    return pl.pallas_call(
        functools.partial(_mix_out_kernel, tm=tm, seq=seq),
        grid=(n // tm,),
        in_specs=[row(d_model), row(d), row(d), row(d), row(gd.shape[1]),
                  row(d_conv), cp, cn, row(d_conv), row(d_model), row(d_model),
                  _full(conv_w.shape), _full(conv_b.shape), _full(w_a.shape), _full(g2.shape),
                  _full(gn_w.shape), _full(gn_b.shape), _full(bd.shape),
                  _full(w_b.shape), _full(w_out.shape), _full(g_post.shape)],
        out_specs=row(d_model),
        out_shape=jax.ShapeDtypeStruct((n, d_model), F32),
        compiler_params=_params(1),
        name="mix_out",
    )(x, yf, yb, bonus, gd, cghc, cghc, cghc, bgate, sgc, sgr, conv_w, conv_b, w_a, g2, gn_w,
      gn_b, bd, w_b, w_out, g_post)


def _ffn_kernel(x_ref, xp_ref, xn_ref, p_ref, gpre_ref, wu_ref, cw_ref, cb_ref, wd_ref, gf_ref, wp_ref,
                wg_ref, gp_ref, o_ref, act_s, *, tm, seq, d_ff, col_chunk):
    i = pl.program_id(0)
    first = (i * tm) % seq == 0
    last = ((i + 1) * tm) % seq == 0
    g = gpre_ref[...]
    u_prev = jnp.where(first, 0.0, _rms(xp_ref[...], g))
    u_next = jnp.where(last, 0.0, _rms(xn_ref[...], g))
    u = jnp.concatenate([u_prev, _rms(x_ref[...], g), u_next], axis=0).astype(BF16)
    m = tm + 2 * SUBLANES
    for c0 in range(0, d_ff, col_chunk):
        conv = []
        for off in (c0, d_ff + c0):
            cols = slice(off, off + col_chunk)
            h = _dot(u, wu_ref[:, cols])
            conv.append(pltpu.roll(h, 1, 0) * cw_ref[0:1, cols] + h * cw_ref[1:2, cols]
                        + pltpu.roll(h, m - 1, 0) * cw_ref[2:3, cols] + cb_ref[:, cols])
        hg, hv = conv
        gelu = 0.5 * hg * (1.0 + jnp.tanh(GELU_C * (hg + 0.044715 * (hg * hg * hg))))
        act_s[:, c0:c0 + col_chunk] = (gelu * hv)[SUBLANES:SUBLANES + tm].astype(BF16)
    f = _dot(act_s[...], wd_ref[...])
    x = x_ref[...] + _rms(f, gf_ref[...])
    gate = _sigmoid(_dot(x.astype(BF16), wg_ref[...]))
    pe = _dot(p_ref[...].astype(BF16), wp_ref[...])
    o_ref[...] = x + _rms(gate * pe, gp_ref[...])


def _ffn(x, p, g_pre, w_up, conv_w, conv_b, w_down, g_ffn, w_ple, w_gate, g_ple, *, layer, tm, seq,
         col_chunk):
    n, d_model = x.shape
    d_ff = w_down.shape[0]
    row = lambda w: pl.BlockSpec((tm, w), lambda i: (i, 0))
    xp, xn = _halo_specs(tm, d_model, SUBLANES, n)
    const = lambda a: pl.BlockSpec(a.shape, lambda i: (0,) * a.ndim, pipeline_mode=pl.Buffered(1))
    return pl.pallas_call(
        functools.partial(_ffn_kernel, tm=tm, seq=seq, d_ff=d_ff, col_chunk=col_chunk),
        grid=(n // tm,),
        in_specs=[row(d_model), xp, xn, pl.BlockSpec((None, tm, p.shape[2]), lambda i: (layer, i, 0)),
                  const(g_pre), const(w_up), const(conv_w),
                  const(conv_b), const(w_down), const(g_ffn), const(w_ple), const(w_gate), const(g_ple)],
        out_specs=row(d_model),
        out_shape=jax.ShapeDtypeStruct((n, d_model), F32),
        scratch_shapes=[pltpu.VMEM((tm, d_ff), BF16)],
        compiler_params=_params(1),
        name="ffn",
    )(x, x, x, p, g_pre, w_up, conv_w, conv_b, w_down, g_ffn, w_ple, w_gate, g_ple)


def _tiles(seq):
    tm = min(MXU_WIDTH, seq)
    tm_in = min(2 * MXU_WIDTH, seq)
    tm_mix = min(4 * MXU_WIDTH, seq)
    tb = min(2 * MXU_WIDTH, seq)
    return tm, tm_in, tm_mix, tb


def _layer_weights(i, norm_mix_pre, norm_mix_post, norm_ffn_pre, norm_ffn_post, norm_ple_post, w_in,
                   conv_w, conv_b, w_branch_a, shift_mu, decay_w0, decay_w2, iclr_a0, iclr_a2, gate_g2,
                   k_k, k_a, r_k, gn_w, gn_b, w_branch_b, w_out, w_up, ffn_conv_w, ffn_conv_b, w_down,
                   w_ple, w_ple_gate):
    d_rwkv = k_k.shape[1]
    lora = decay_w2.shape[2]
    row = lambda a: a[i].reshape(1, -1)
    zeros = jnp.zeros((lora, d_rwkv), F32)
    lowrank = [jnp.concatenate([jnp.concatenate([decay_w2[i, d], zeros], axis=1),
                                jnp.concatenate([zeros, iclr_a2[i, d]], axis=1)], axis=0).astype(BF16)
               for d in range(2)]
    head_id = jnp.arange(d_rwkv) // HEAD
    return dict(
        g_mix_pre=row(norm_mix_pre), g_mix_post=row(norm_mix_post), g_ffn_pre=row(norm_ffn_pre),
        g_ffn_post=row(norm_ffn_post), g_ple_post=row(norm_ple_post),
        w_in=w_in[i].astype(BF16), conv_w=conv_w[i], conv_b=row(conv_b),
        w_a=w_branch_a[i].astype(BF16),
        mu=shift_mu[i][:, None, :], w0=decay_w0[i][:, None, :], a0=iclr_a0[i][:, None, :],
        lowrank=jnp.stack(lowrank), g2=gate_g2[i].astype(BF16), k_k=row(k_k), k_a=row(k_a),
        r_k=r_k[i].reshape(1, -1), gn_w=row(gn_w), gn_b=row(gn_b),
        bd=(head_id[:, None] == head_id[None, :]).astype(BF16),
        w_b=w_branch_b[i].astype(BF16), w_out=w_out[i].astype(BF16), w_up=w_up[i].astype(BF16),
        ffn_conv_w=ffn_conv_w[i], ffn_conv_b=row(ffn_conv_b), w_down=w_down[i].astype(BF16),
        w_ple=w_ple[i].astype(BF16), w_gate=w_ple_gate[i].astype(BF16))


def _layer(x, p, lw, masks, *, layer, bsz, seq):
    tm, tm_in, tm_mix, tb = _tiles(seq)
    d_conv = lw["conv_w"].shape[1]
    d_rwkv = lw["k_k"].shape[1]
    d_z = lw["mu"].shape[2]
    d_g = lw["g2"].shape[0]
    d_ff = lw["w_down"].shape[0]
    cghc, bgate, r, k, v, z, gd, sgc, sgr = _in_proj(
        x, lw["g_mix_pre"], lw["w_in"], tm=tm_in, d_conv=d_conv, d_rwkv=d_rwkv, d_z=2 * d_z, d_g=d_g)
    yf, yb, bonus = _wkv(r, k, v, z, lw["mu"], lw["w0"], lw["a0"], lw["lowrank"], lw["k_k"], lw["k_a"],
                         lw["r_k"], lw["bd"], masks, bsz=bsz, seq=seq, tb=tb)
    x = _mix_out(x, yf, yb, bonus, gd, cghc, bgate, sgc, sgr, lw["conv_w"], lw["conv_b"],
                 lw["w_a"], lw["g2"], lw["gn_w"], lw["gn_b"], lw["bd"], lw["w_b"],
                 lw["w_out"], lw["g_mix_post"], tm=tm_mix, seq=seq)
    return _ffn(x, p, lw["g_ffn_pre"], lw["w_up"], lw["ffn_conv_w"], lw["ffn_conv_b"], lw["w_down"],
                lw["g_ffn_post"], lw["w_ple"], lw["w_gate"], lw["g_ple_post"], layer=layer, tm=tm,
                seq=seq, col_chunk=math.gcd(d_ff, MXU_WIDTH))


def kernel(x_prompt, x_sample, p_prompt, p_sample, norm_mix_pre, norm_mix_post, norm_ffn_pre, norm_ffn_post, norm_ple_post, w_in, conv_w, conv_b, w_branch_a, shift_mu, decay_w0, decay_w2, iclr_a0, iclr_a2, gate_g2, k_k, k_a, r_k, gn_w, gn_b, w_branch_b, w_out, w_up, ffn_conv_w, ffn_conv_b, w_down, w_ple, w_ple_gate):
    weights = (norm_mix_pre, norm_mix_post, norm_ffn_pre, norm_ffn_post, norm_ple_post, w_in, conv_w,
               conv_b, w_branch_a, shift_mu, decay_w0, decay_w2, iclr_a0, iclr_a2, gate_g2, k_k, k_a,
               r_k, gn_w, gn_b, w_branch_b, w_out, w_up, ffn_conv_w, ffn_conv_b, w_down, w_ple,
               w_ple_gate)
    layers = [_layer_weights(i, *weights) for i in range(w_in.shape[0])]
    masks = _wkv_masks()
    outs = []
    for x, p in ((x_prompt, p_prompt), (x_sample, p_sample)):
        bsz, seq, d_model = x.shape
        y = x.reshape(bsz * seq, d_model)
        p = p.reshape(p.shape[0], bsz * seq, -1)
        for i, lw in enumerate(layers):
            y = _layer(y, p, lw, masks, layer=i, bsz=bsz, seq=seq)
        outs.append(y.reshape(bsz, seq, d_model))
    return tuple(outs)
```

```python
import functools
import math

import jax
import jax.numpy as jnp
from jax import lax
from jax.experimental import pallas as pl
from jax.experimental.pallas import tpu as pltpu

F32 = jnp.float32
BF16 = jnp.bfloat16

MXU_WIDTH = 256
SUBLANES = 8
BF16_ROWS = 16
VMEM_LIMIT_BYTES = 56 * 1024 * 1024

HEAD = 64
PAIR = 2 * HEAD
CHUNK = 64
CHUNKS_PER_STEP = 2
NORM_EPS = 1e-6
GN_EPS = HEAD * 1e-5
DECAY_SCALE = math.exp(-0.5)
GELU_C = math.sqrt(2.0 / math.pi)


def _sigmoid(x):
    return 1.0 / (1.0 + jnp.exp(-x))


def _rms(x, g):
    return x * lax.rsqrt(jnp.mean(x * x, axis=-1, keepdims=True) + NORM_EPS) * g


def _dot(a, b):
    return jnp.dot(a, b, preferred_element_type=F32)


def _dot_nt(a, b):
    return lax.dot_general(a, b, (((1,), (1,)), ((), ())), preferred_element_type=F32)


def _dot_tn(a, b):
    return lax.dot_general(a, b, (((0,), (0,)), ((), ())), preferred_element_type=F32)


def _split2(x):
    hi = x.astype(BF16)
    lo = (x - hi.astype(F32)).astype(BF16)
    return hi, lo


def _head_sum(x, bd):
    return _dot(x.astype(BF16), bd)


def _params(n_axes):
    return pltpu.CompilerParams(dimension_semantics=("arbitrary",) * n_axes,
                                vmem_limit_bytes=VMEM_LIMIT_BYTES)


def _full(shape):
    nd = len(shape)
    return pl.BlockSpec(shape, lambda *_: (0,) * nd)


def _in_proj_kernel(x_ref, g_ref, w_ref, cghc_ref, bg_ref, r_ref, k_ref, v_ref, z_ref, gd_ref,
                    *, d_conv, d_rwkv, d_z, d_g):
    u = _rms(x_ref[...], g_ref[...]).astype(BF16)
    o = 0
    hbc = _dot(u, w_ref[:, o:o + 3 * d_conv])
    cghc_ref[...] = (hbc[:, 2 * d_conv:] * hbc[:, :d_conv]).astype(BF16)
    bg_ref[...] = hbc[:, d_conv:2 * d_conv].astype(BF16)
    o += 3 * d_conv
    for ref in (r_ref, k_ref, v_ref):
        ref[...] = _dot(u, w_ref[:, o:o + d_rwkv])
        o += d_rwkv
    z_ref[...] = _dot(u, w_ref[:, o:o + d_z])
    o += d_z
    gd_ref[...] = _dot(u, w_ref[:, o:o + d_g])


def _in_proj(x, g, w_in, *, tm, d_conv, d_rwkv, d_z, d_g):
    n, d_model = x.shape
    cols = w_in.shape[1]
    row = lambda w: pl.BlockSpec((tm, w), lambda i: (i, 0))
    widths = (d_conv, d_conv, d_rwkv, d_rwkv, d_rwkv, d_z, d_g)
    dtypes = (BF16, BF16, F32, F32, F32, F32, F32)
    return pl.pallas_call(
        functools.partial(_in_proj_kernel, d_conv=d_conv, d_rwkv=d_rwkv, d_z=d_z, d_g=d_g),
        grid=(n // tm,),
        in_specs=[row(d_model), _full((1, d_model)), _full((d_model, cols))],
        out_specs=[row(w) for w in widths],
        out_shape=[jax.ShapeDtypeStruct((n, w), dt) for w, dt in zip(widths, dtypes)],
        compiler_params=_params(1),
        name="in_proj",
    )(x, g, w_in)


def _wkv_masks():
    ti = lax.broadcasted_iota(jnp.int32, (CHUNK, PAIR), 0)
    lane = lax.broadcasted_iota(jnp.int32, (CHUNK, PAIR), 1)
    si = lane & (HEAD - 1)
    li = lax.broadcasted_iota(jnp.int32, (CHUNK, CHUNK), 0)
    lj = lax.broadcasted_iota(jnp.int32, (CHUNK, CHUNK), 1)
    head_lo = (lane < HEAD).astype(BF16)
    cum, aa, lvl0, lvls = [], [], [], []
    for reverse in (False, True):
        strict = ((si > ti) if reverse else (si < ti)).astype(BF16)
        incl = ((si >= ti) if reverse else (si <= ti)).astype(BF16)
        cum.append(((lj >= li) if reverse else (lj <= li)).astype(BF16))
        aa.append(jnp.concatenate([jnp.concatenate([strict, strict], axis=1),
                                   jnp.concatenate([incl, incl], axis=1)], axis=0))
        levels = []
        sz = 1
        while sz < CHUNK:
            same_blk = (ti & -(2 * sz)) == (si & -(2 * sz))
            t_hi, s_hi = (ti & sz) != 0, (si & sz) != 0
            levels.append((same_blk & ((~t_hi & s_hi) if reverse else (t_hi & ~s_hi))).astype(BF16))
            sz *= 2
        lvl0.append(levels[0])
        lvls.append(jnp.stack([jnp.concatenate([m * head_lo, m * (1 - head_lo)], axis=0)
                               for m in levels[1:]]))
    bi = lax.broadcasted_iota(jnp.int32, (PAIR, PAIR), 0)
    bj = lax.broadcasted_iota(jnp.int32, (PAIR, PAIR), 1)
    return dict(cum=jnp.stack(cum), aa=jnp.stack(aa), lvl0=jnp.stack(lvl0), lvls=jnp.stack(lvls),
                eye=(si == ti).astype(BF16), heads=jnp.stack([head_lo, 1 - head_lo]),
                same_head=((bi < HEAD) == (bj < HEAD)).astype(F32))


def _wkv_features(z_ref, mu, w2, zc_s, *, reverse, tb):
    z = z_ref[...]
    rows = lax.broadcasted_iota(jnp.int32, z.shape, 0)
    carry = jnp.broadcast_to(zc_s[0:1, :], z.shape)
    if reverse:
        zs = jnp.where(rows == tb - 1, carry, pltpu.roll(z, tb - 1, 0))
        zc_s[...] = jnp.broadcast_to(z[0:1, :], zc_s.shape)
    else:
        zs = jnp.where(rows == 0, carry, pltpu.roll(z, 1, 0))
        zc_s[...] = jnp.broadcast_to(z[tb - 1:tb, :], zc_s.shape)
    zm = z + mu * (zs - z)
    lanes = lax.broadcasted_iota(jnp.int32, z.shape, 1)
    feat = jnp.where(lanes < HEAD, jnp.tanh(zm), zm).astype(BF16)
    return _dot(feat, w2)


def _wkv_prep(lo, k, w0, a0, k_k, k_a, bd):
    d = k.shape[1]
    lw = -DECAY_SCALE * _sigmoid(w0 + lo[:, :d])
    a = _sigmoid(a0 + lo[:, d:])
    kkr = k * k_k
    kkn = kkr * lax.rsqrt(jnp.maximum(_head_sum(kkr * kkr, bd), 1e-24))
    return lw, kkn, kkn * a, k * (1.0 + (a - 1.0) * k_a)


def _wkv_kernel(rf_ref, kf_ref, vf_ref, zf_ref, rb_ref, kb_ref, vb_ref, zb_ref, mu_ref, w0_ref, a0_ref,
                w2_ref, kk_ref, ka_ref, rk_ref, bd_ref, cum_ref, aa_ref, lvl0_ref, lvls_ref, eye_ref,
                heads_ref, same_ref, yf_ref, yb_ref, bonus_ref, state_s, zc_s, *, tb, n_pairs):
    nch = tb // CHUNK
    r_refs, k_refs, v_refs, z_refs, y_refs = ((rf_ref, rb_ref), (kf_ref, kb_ref), (vf_ref, vb_ref),
                                              (zf_ref, zb_ref), (yf_ref, yb_ref))

    @pl.when(pl.program_id(1) == 0)
    def _():
        state_s[...] = jnp.zeros_like(state_s)
        zc_s[...] = jnp.zeros_like(zc_s)

    lo = [_wkv_features(z_refs[dr], mu_ref[dr], w2_ref[dr], zc_s.at[dr], reverse=bool(dr), tb=tb)
          for dr in range(2)]
    edge = (CHUNK - 1, 0)

    lane_lo = lax.broadcasted_iota(jnp.int32, (CHUNK, PAIR), 1) < HEAD

    def stack(x):
        if x.dtype == F32:
            return jnp.concatenate([jnp.where(lane_lo, x, 0.0), jnp.where(lane_lo, 0.0, x)],
                                   axis=0).astype(BF16)
        return jnp.concatenate([x * heads_ref[0], x * heads_ref[1]], axis=0)

    group = math.gcd(nch, CHUNKS_PER_STEP)
    span = group * CHUNK
    n_groups = nch // group
    units = [(dr, p, j) for j in range(group) for dr in range(2) for p in range(n_pairs)]
    halves = (units[:len(units) // 2], units[len(units) // 2:])
    n_levels = lvls_ref.shape[1] + 1
    state = [state_s[i] for i in range(2 * n_pairs)]

    def rows(g, dr, j):
        base = g * span if dr == 0 else tb - (g + 1) * span
        off = base + (j if dr == 0 else group - 1 - j) * CHUNK
        return slice(off, off + CHUNK)

    def head(g, out):
        prep = {}
        for dr in range(2):
            blk = slice(rows(g, dr, 0 if dr == 0 else group - 1).start,
                        rows(g, dr, group - 1 if dr == 0 else 0).stop)
            prep[dr] = _wkv_prep(lo[dr][blk], k_refs[dr][blk, :], w0_ref[dr], a0_ref[dr], kk_ref[...],
                                 ka_ref[...], bd_ref[...])
            if dr == 0:
                rk_sum = _head_sum(rf_ref[blk, :] * kf_ref[blk, :] * rk_ref[...], bd_ref[...])
                bonus_ref[blk, :] = (rk_sum * vf_ref[blk, :]).astype(BF16)
            yield

        def take(which, dr, p, j):
            off = (j if dr == 0 else group - 1 - j) * CHUNK
            return prep[dr][which][off:off + CHUNK, p * PAIR:(p + 1) * PAIR]

        def load(refs, dr, p, j):
            return refs[dr][rows(g, dr, j), p * PAIR:(p + 1) * PAIR]

        for key in ("cw", "rw", "kkw", "v", "a_ab", "a_rb", "a_kv", "tinv", "w_end", "bk_end"):
            out[key] = {}
        for part in halves:
            for un in part:
                dr, p, j = un
                h1, h2 = _split2(take(0, dr, p, j))
                cs = _dot(cum_ref[dr], jnp.concatenate([h1, h2], axis=1))
                out["cw"][un] = cs[:, :PAIR] + cs[:, PAIR:]
            yield
        for part in halves:
            for un in part:
                dr, p, j = un
                cw = out["cw"].pop(un)
                e_pos, e_neg = jnp.exp(cw), jnp.exp(-cw)
                rw = (load(r_refs, dr, p, j) * e_pos).astype(BF16)
                kkw = (take(1, dr, p, j) * jnp.exp(cw - take(0, dr, p, j))).astype(BF16)
                binv = take(2, dr, p, j) * e_neg
                kinv = take(3, dr, p, j) * e_neg
                w_end = e_pos[edge[dr]:edge[dr] + 1, :]
                out["rw"][un], out["kkw"][un], out["w_end"][un] = rw, kkw, w_end
                out["v"][un] = load(v_refs, dr, p, j).astype(BF16)
                out["bk_end"][un] = jnp.concatenate([binv * w_end, kinv * w_end], axis=0).astype(BF16)
                aa = (_dot_nt(jnp.concatenate([kkw, rw], axis=0),
                              jnp.concatenate([stack(binv), stack(kinv)], axis=0)).astype(BF16)
                      * aa_ref[dr])
                a_ab = aa[:CHUNK, :PAIR]
                out["a_ab"][un] = a_ab
                out["a_rb"][un] = aa[CHUNK:, :PAIR]
                out["a_kv"][un] = aa[:, PAIR:]
                out["tinv"][un] = eye_ref[...] - a_ab * lvl0_ref[dr]
            yield

    def tail(g, h):
        nonlocal state
        av, tt = {}, {}
        for part in halves:
            for un in part:
                av[un] = _dot(h["a_kv"][un], stack(h["v"][un]))
            yield
        for part in halves:
            for un in part:
                tt[un] = _dot(h["tinv"][un],
                              jnp.concatenate([stack(h["kkw"][un]), stack(av[un][:CHUNK])], axis=1))
            yield
        for jj in range(group):
            uns = [un for un in units if un[2] == jj]
            ps = [_dot_nt(jnp.concatenate([tt[un][:, :PAIR].astype(BF16), h["rw"][un]], axis=0),
                          s.astype(BF16)) for un, s in zip(uns, state)]
            yield
            u = [-(x[:CHUNK] + tt[un][:, PAIR:]) for un, x in zip(uns, ps)]
            y = [x[CHUNK:] + _dot(h["a_rb"][un], stack(w)) + av[un][CHUNK:]
                 for un, x, w in zip(uns, ps, u)]
            upd = [_dot_tn(jnp.concatenate([x.astype(BF16), h["v"][un]], axis=0), h["bk_end"][un])
                   for un, x in zip(uns, u)]
            yield
            for (dr, p, j), x in zip(uns, y):
                y_refs[dr][rows(g, dr, j), p * PAIR:(p + 1) * PAIR] = x.astype(BF16)
            state = [s * h["w_end"][un] + x * same_ref[...] for un, s, x in zip(uns, state, upd)]
            yield

    def level(h, lvl):
        sz = 1 << lvl
        m1, sel = {}, {}
        for un in units:
            dr = un[0]
            t = h["tinv"][un]
            if sz % BF16_ROWS == 0:
                sel[un] = [r0 for r0 in range(0, CHUNK, sz) if ((r0 & sz) != 0) != bool(dr)]
                t = jnp.concatenate([t[r0:r0 + sz] for r0 in sel[un]], axis=0)
            a2 = jnp.concatenate([h["a_ab"][un], h["a_ab"][un]], axis=0)
            m1[un] = _dot(t, a2 * lvls_ref[dr, lvl - 1])
        yield
        for un in units:
            t = h["tinv"][un]
            m2 = _dot(m1[un].astype(BF16), stack(t)).astype(BF16)
            if un in sel:
                blocks = [t[r0:r0 + sz] for r0 in range(0, CHUNK, sz)]
                for i, r0 in enumerate(sel[un]):
                    blocks[r0 // sz] = blocks[r0 // sz] - m2[i * sz:(i + 1) * sz]
                h["tinv"][un] = jnp.concatenate(blocks, axis=0)
            else:
                h["tinv"][un] = t - m2
        yield

    def advance(gens):
        for gen in gens:
            next(gen, None)

    def drain(gens):
        for gen in gens:
            for _ in gen:
                pass

    heads = [dict() for _ in range(n_groups)]
    drain([head(0, heads[0])])
    side = []
    for g in range(n_groups):
        h = heads[g]
        if g + 1 < n_groups:
            side.append(head(g + 1, heads[g + 1]))
        for lvl in range(1, n_levels):
            for _ in level(h, lvl):
                advance(side)
        drain(side)
        side = [tail(g, h)]
    drain(side)
    for i, s in enumerate(state):
        state_s[i] = s


def _wkv(r, k, v, z, mu, w0, a0, w2, k_k, k_a, r_k, bd, masks, *, bsz, seq, tb):
    n, d = r.shape
    nt = seq // tb
    n_pairs = d // PAIR
    fmap = lambda b, i: (b * nt + i, 0)
    bmap = lambda b, i: (b * nt + nt - 1 - i, 0)
    fblk, bblk = pl.BlockSpec((tb, d), fmap), pl.BlockSpec((tb, d), bmap)
    zf = pl.BlockSpec((tb, PAIR), fmap)
    zb = pl.BlockSpec((tb, PAIR), lambda b, i: (bmap(b, i)[0], 1))
    consts = [masks[key] for key in ("cum", "aa", "lvl0", "lvls", "eye", "heads", "same_head")]
    return pl.pallas_call(
        functools.partial(_wkv_kernel, tb=tb, n_pairs=n_pairs),
        grid=(bsz, nt),
        in_specs=[fblk, fblk, fblk, zf, bblk, bblk, bblk, zb,
                  _full(mu.shape), _full(w0.shape), _full(a0.shape), _full(w2.shape),
                  _full(k_k.shape), _full(k_a.shape), _full(r_k.shape), _full(bd.shape)]
        + [_full(m.shape) for m in consts],
        out_specs=[fblk, bblk, fblk],
        out_shape=[jax.ShapeDtypeStruct((n, d), BF16)] * 3,
        scratch_shapes=[pltpu.VMEM((2 * n_pairs, PAIR, PAIR), F32), pltpu.VMEM((2, SUBLANES, PAIR), F32)],
        compiler_params=_params(2),
        name="wkv",
    )(r, k, v, z, r, k, v, z, mu, w0, a0, w2, k_k, k_a, r_k, bd, *consts)


def _halo_specs(tm, width, rows_per_blk, n_rows):
    nb = tm // rows_per_blk
    last_blk = n_rows // rows_per_blk - 1
    prev = pl.BlockSpec((rows_per_blk, width), lambda i: (jnp.maximum(i * nb - 1, 0), 0))
    nxt = pl.BlockSpec((rows_per_blk, width), lambda i: (jnp.minimum((i + 1) * nb, last_blk), 0))
    return prev, nxt


def _mix_out_kernel(x_ref, yf_ref, yb_ref, bonus_ref, gd_ref, c_ref, cp_ref, cn_ref, bg_ref,
                    gpre_ref, wgate_ref, cw_ref, cb_ref, wa_ref, g2_ref, gnw_ref, gnb_ref,
                    bd_ref, wb_ref, wo_ref, g_ref, o_ref, *, tm, seq):
    i = pl.program_id(0)
    first = (i * tm) % seq == 0
    last = ((i + 1) * tm) % seq == 0
    halo_p, halo_n = cp_ref[...], cn_ref[...]
    c = jnp.concatenate([jnp.where(first, jnp.zeros_like(halo_p), halo_p), c_ref[...],
                         jnp.where(last, jnp.zeros_like(halo_n), halo_n)], axis=0).astype(F32)
    m = tm + 2 * BF16_ROWS
    conv = (pltpu.roll(c, 1, 0) * cw_ref[0:1, :] + c * cw_ref[1:2, :]
            + pltpu.roll(c, m - 1, 0) * cw_ref[2:3, :] + cb_ref[...])[BF16_ROWS:BF16_ROWS + tm]
    y_conv = _dot((bg_ref[...].astype(F32) * conv).astype(BF16), wa_ref[...])
    bd = bd_ref[...]
    y = yf_ref[...].astype(F32) + yb_ref[...].astype(F32)
    mean = _head_sum(y, bd) * (1.0 / HEAD)
    yc = y - mean
    var = _head_sum(yc * yc, bd) * (1.0 / HEAD)
    yn = yc * lax.rsqrt(var + GN_EPS) * gnw_ref[...] + gnb_ref[...]
    gate = _dot(_sigmoid(gd_ref[...]).astype(BF16), g2_ref[...])
    y_rwkv = _dot(((yn + bonus_ref[...].astype(F32)) * gate).astype(BF16), wb_ref[...])
    d_model = x_ref.shape[1]
    u = _rms(x_ref[...], gpre_ref[...]).astype(BF16)
    merged = (_sigmoid(_dot(u, wgate_ref[:, :d_model])) * y_conv
              + _sigmoid(_dot(u, wgate_ref[:, d_model:])) * y_rwkv)
    m = _dot(merged.astype(BF16), wo_ref[...])
    o_ref[...] = x_ref[...] + _rms(m, g_ref[...])


def _mix_out(x, yf, yb, bonus, gd, cghc, bgate, g_pre, w_gates, conv_w, conv_b, w_a, g2, gn_w, gn_b,
             bd, w_b, w_out, g_post, *, tm, seq):
    n, d_model = x.shape
    d = yf.shape[1]
    d_conv = cghc.shape[1]
    row = lambda w: pl.BlockSpec((tm, w), lambda i: (i, 0))
    cp, cn = _halo_specs(tm, d_conv, BF16_ROWS, n)
    return pl.pallas_call(
        functools.partial(_mix_out_kernel, tm=tm, seq=seq),
        grid=(n // tm,),
        in_specs=[row(d_model), row(d), row(d), row(d), row(gd.shape[1]),
                  row(d_conv), cp, cn, row(d_conv), _full(g_pre.shape), _full(w_gates.shape),
                  _full(conv_w.shape), _full(conv_b.shape), _full(w_a.shape), _full(g2.shape),
                  _full(gn_w.shape), _full(gn_b.shape), _full(bd.shape),
                  _full(w_b.shape), _full(w_out.shape), _full(g_post.shape)],
        out_specs=row(d_model),
        out_shape=jax.ShapeDtypeStruct((n, d_model), F32),
        compiler_params=_params(1),
        name="mix_out",
    )(x, yf, yb, bonus, gd, cghc, cghc, cghc, bgate, g_pre, w_gates, conv_w, conv_b, w_a, g2, gn_w,
      gn_b, bd, w_b, w_out, g_post)


def _ffn_kernel(x_ref, xp_ref, xn_ref, p_ref, gpre_ref, wu_ref, cw_ref, cb_ref, wd_ref, gf_ref, wp_ref,
                wg_ref, gp_ref, o_ref, act_s, *, tm, seq, d_ff, col_chunk):
    i = pl.program_id(0)
    first = (i * tm) % seq == 0
    last = ((i + 1) * tm) % seq == 0
    g = gpre_ref[...]
    u_prev = jnp.where(first, 0.0, _rms(xp_ref[...], g))
    u_next = jnp.where(last, 0.0, _rms(xn_ref[...], g))
    u = jnp.concatenate([u_prev, _rms(x_ref[...], g), u_next], axis=0).astype(BF16)
    m = tm + 2 * SUBLANES
    for c0 in range(0, d_ff, col_chunk):
        conv = []
        for off in (c0, d_ff + c0):
            cols = slice(off, off + col_chunk)
            h = _dot(u, wu_ref[:, cols])
            conv.append(pltpu.roll(h, 1, 0) * cw_ref[0:1, cols] + h * cw_ref[1:2, cols]
                        + pltpu.roll(h, m - 1, 0) * cw_ref[2:3, cols] + cb_ref[:, cols])
        hg, hv = conv
        gelu = 0.5 * hg * (1.0 + jnp.tanh(GELU_C * (hg + 0.044715 * (hg * hg * hg))))
        act_s[:, c0:c0 + col_chunk] = (gelu * hv)[SUBLANES:SUBLANES + tm].astype(BF16)
    f = _dot(act_s[...], wd_ref[...])
    x = x_ref[...] + _rms(f, gf_ref[...])
    gate = _sigmoid(_dot(x.astype(BF16), wg_ref[...]))
    pe = _dot(p_ref[...].astype(BF16), wp_ref[...])
    o_ref[...] = x + _rms(gate * pe, gp_ref[...])


def _ffn(x, p, g_pre, w_up, conv_w, conv_b, w_down, g_ffn, w_ple, w_gate, g_ple, *, layer, tm, seq,
         col_chunk):
    n, d_model = x.shape
    d_ff = w_down.shape[0]
    row = lambda w: pl.BlockSpec((tm, w), lambda i: (i, 0))
    xp, xn = _halo_specs(tm, d_model, SUBLANES, n)
    const = lambda a: pl.BlockSpec(a.shape, lambda i: (0,) * a.ndim, pipeline_mode=pl.Buffered(1))
    return pl.pallas_call(
        functools.partial(_ffn_kernel, tm=tm, seq=seq, d_ff=d_ff, col_chunk=col_chunk),
        grid=(n // tm,),
        in_specs=[row(d_model), xp, xn, pl.BlockSpec((None, tm, p.shape[2]), lambda i: (layer, i, 0)),
                  const(g_pre), const(w_up), const(conv_w),
                  const(conv_b), const(w_down), const(g_ffn), const(w_ple), const(w_gate), const(g_ple)],
        out_specs=row(d_model),
        out_shape=jax.ShapeDtypeStruct((n, d_model), F32),
        scratch_shapes=[pltpu.VMEM((tm, d_ff), BF16)],
        compiler_params=_params(1),
        name="ffn",
    )(x, x, x, p, g_pre, w_up, conv_w, conv_b, w_down, g_ffn, w_ple, w_gate, g_ple)


def _tiles(seq):
    tm = min(MXU_WIDTH, seq)
    tm_in = min(2 * MXU_WIDTH, seq)
    tm_mix = min(4 * MXU_WIDTH, seq)
    tb = min(2 * MXU_WIDTH, seq)
    return tm, tm_in, tm_mix, tb


def _layer_weights(i, norm_mix_pre, norm_mix_post, norm_ffn_pre, norm_ffn_post, norm_ple_post, w_in,
                   conv_w, conv_b, w_branch_a, shift_mu, decay_w0, decay_w2, iclr_a0, iclr_a2, gate_g2,
                   k_k, k_a, r_k, gn_w, gn_b, w_branch_b, w_out, w_up, ffn_conv_w, ffn_conv_b, w_down,
                   w_ple, w_ple_gate):
    d_rwkv = k_k.shape[1]
    lora = decay_w2.shape[2]
    row = lambda a: a[i].reshape(1, -1)
    zeros = jnp.zeros((lora, d_rwkv), F32)
    lowrank = [jnp.concatenate([jnp.concatenate([decay_w2[i, d], zeros], axis=1),
                                jnp.concatenate([zeros, iclr_a2[i, d]], axis=1)], axis=0).astype(BF16)
               for d in range(2)]
    head_id = jnp.arange(d_rwkv) // HEAD
    d_model = w_in.shape[1]
    return dict(
        g_mix_pre=row(norm_mix_pre), g_mix_post=row(norm_mix_post), g_ffn_pre=row(norm_ffn_pre),
        g_ffn_post=row(norm_ffn_post), g_ple_post=row(norm_ple_post),
        w_in=w_in[i, :, :-2 * d_model].astype(BF16), w_gates=w_in[i, :, -2 * d_model:].astype(BF16),
        conv_w=conv_w[i], conv_b=row(conv_b),
        w_a=w_branch_a[i].astype(BF16),
        mu=shift_mu[i][:, None, :], w0=decay_w0[i][:, None, :], a0=iclr_a0[i][:, None, :],
        lowrank=jnp.stack(lowrank), g2=gate_g2[i].astype(BF16), k_k=row(k_k), k_a=row(k_a),
        r_k=r_k[i].reshape(1, -1), gn_w=row(gn_w), gn_b=row(gn_b),
        bd=(head_id[:, None] == head_id[None, :]).astype(BF16),
        w_b=w_branch_b[i].astype(BF16), w_out=w_out[i].astype(BF16), w_up=w_up[i].astype(BF16),
        ffn_conv_w=ffn_conv_w[i], ffn_conv_b=row(ffn_conv_b), w_down=w_down[i].astype(BF16),
        w_ple=w_ple[i].astype(BF16), w_gate=w_ple_gate[i].astype(BF16))


def _layer(x, p, lw, masks, *, layer, bsz, seq):
    tm, tm_in, tm_mix, tb = _tiles(seq)
    d_conv = lw["conv_w"].shape[1]
    d_rwkv = lw["k_k"].shape[1]
    d_z = lw["mu"].shape[2]
    d_g = lw["g2"].shape[0]
    d_ff = lw["w_down"].shape[0]
    cghc, bgate, r, k, v, z, gd = _in_proj(
        x, lw["g_mix_pre"], lw["w_in"], tm=tm_in, d_conv=d_conv, d_rwkv=d_rwkv, d_z=2 * d_z, d_g=d_g)
    yf, yb, bonus = _wkv(r, k, v, z, lw["mu"], lw["w0"], lw["a0"], lw["lowrank"], lw["k_k"], lw["k_a"],
                         lw["r_k"], lw["bd"], masks, bsz=bsz, seq=seq, tb=tb)
    x = _mix_out(x, yf, yb, bonus, gd, cghc, bgate, lw["g_mix_pre"], lw["w_gates"], lw["conv_w"], lw["conv_b"],
                 lw["w_a"], lw["g2"], lw["gn_w"], lw["gn_b"], lw["bd"], lw["w_b"],
                 lw["w_out"], lw["g_mix_post"], tm=tm_mix, seq=seq)
    return _ffn(x, p, lw["g_ffn_pre"], lw["w_up"], lw["ffn_conv_w"], lw["ffn_conv_b"], lw["w_down"],
                lw["g_ffn_post"], lw["w_ple"], lw["w_gate"], lw["g_ple_post"], layer=layer, tm=tm,
                seq=seq, col_chunk=math.gcd(d_ff, MXU_WIDTH))


def kernel(x_prompt, x_sample, p_prompt, p_sample, norm_mix_pre, norm_mix_post, norm_ffn_pre, norm_ffn_post, norm_ple_post, w_in, conv_w, conv_b, w_branch_a, shift_mu, decay_w0, decay_w2, iclr_a0, iclr_a2, gate_g2, k_k, k_a, r_k, gn_w, gn_b, w_branch_b, w_out, w_up, ffn_conv_w, ffn_conv_b, w_down, w_ple, w_ple_gate):
    weights = (norm_mix_pre, norm_mix_post, norm_ffn_pre, norm_ffn_post, norm_ple_post, w_in, conv_w,
               conv_b, w_branch_a, shift_mu, decay_w0, decay_w2, iclr_a0, iclr_a2, gate_g2, k_k, k_a,
               r_k, gn_w, gn_b, w_branch_b, w_out, w_up, ffn_conv_w, ffn_conv_b, w_down, w_ple,
               w_ple_gate)
    layers = [_layer_weights(i, *weights) for i in range(w_in.shape[0])]
    masks = _wkv_masks()
    outs = []
    for x, p in ((x_prompt, p_prompt), (x_sample, p_sample)):
        bsz, seq, d_model = x.shape
        y = x.reshape(bsz * seq, d_model)
        p = p.reshape(p.shape[0], bsz * seq, -1)
        for i, lw in enumerate(layers):
            y = _layer(y, p, lw, masks, layer=i, bsz=bsz, seq=seq)
        outs.append(y.reshape(bsz, seq, d_model))
    return tuple(outs)
```

```python
import functools
import math

import jax
import jax.numpy as jnp
from jax import lax
from jax.experimental import pallas as pl
from jax.experimental.pallas import tpu as pltpu

F32 = jnp.float32
BF16 = jnp.bfloat16

MXU_WIDTH = 256
SUBLANES = 8
BF16_ROWS = 16
VMEM_LIMIT_BYTES = 56 * 1024 * 1024

HEAD = 64
PAIR = 2 * HEAD
CHUNK = 64
CHUNKS_PER_STEP = 2
NORM_EPS = 1e-6
GN_EPS = HEAD * 1e-5
DECAY_SCALE = math.exp(-0.5)
GELU_C = math.sqrt(2.0 / math.pi)


def _sigmoid(x):
    return 1.0 / (1.0 + jnp.exp(-x))


def _rms(x, g):
    return x * lax.rsqrt(jnp.mean(x * x, axis=-1, keepdims=True) + NORM_EPS) * g


def _dot(a, b):
    return jnp.dot(a, b, preferred_element_type=F32)


def _dot_nt(a, b):
    return lax.dot_general(a, b, (((1,), (1,)), ((), ())), preferred_element_type=F32)


def _dot_tn(a, b):
    return lax.dot_general(a, b, (((0,), (0,)), ((), ())), preferred_element_type=F32)


def _split2(x):
    hi = x.astype(BF16)
    lo = (x - hi.astype(F32)).astype(BF16)
    return hi, lo


def _head_sum(x, bd):
    return _dot(x.astype(BF16), bd)


def _params(n_axes):
    return pltpu.CompilerParams(dimension_semantics=("parallel",) + ("arbitrary",) * (n_axes - 1),
                                vmem_limit_bytes=VMEM_LIMIT_BYTES)


def _full(shape):
    nd = len(shape)
    return pl.BlockSpec(shape, lambda *_: (0,) * nd)


def _in_proj_kernel(x_ref, g_ref, w_ref, cghc_ref, bg_ref, r_ref, k_ref, v_ref, z_ref, gd_ref,
                    *, d_conv, d_rwkv, d_z, d_g):
    u = _rms(x_ref[...], g_ref[...]).astype(BF16)
    o = 0
    hbc = _dot(u, w_ref[:, o:o + 3 * d_conv])
    cghc_ref[...] = (hbc[:, 2 * d_conv:] * hbc[:, :d_conv]).astype(BF16)
    bg_ref[...] = hbc[:, d_conv:2 * d_conv].astype(BF16)
    o += 3 * d_conv
    for ref in (r_ref, k_ref, v_ref):
        ref[...] = _dot(u, w_ref[:, o:o + d_rwkv])
        o += d_rwkv
    z_ref[...] = _dot(u, w_ref[:, o:o + d_z])
    o += d_z
    gd_ref[...] = _dot(u, w_ref[:, o:o + d_g])


def _in_proj(x, g, w_in, *, tm, d_conv, d_rwkv, d_z, d_g):
    n, d_model = x.shape
    cols = w_in.shape[1]
    row = lambda w: pl.BlockSpec((tm, w), lambda i: (i, 0))
    widths = (d_conv, d_conv, d_rwkv, d_rwkv, d_rwkv, d_z, d_g)
    dtypes = (BF16, BF16, F32, F32, F32, F32, F32)
    return pl.pallas_call(
        functools.partial(_in_proj_kernel, d_conv=d_conv, d_rwkv=d_rwkv, d_z=d_z, d_g=d_g),
        grid=(n // tm,),
        in_specs=[row(d_model), _full((1, d_model)), _full((d_model, cols))],
        out_specs=[row(w) for w in widths],
        out_shape=[jax.ShapeDtypeStruct((n, w), dt) for w, dt in zip(widths, dtypes)],
        compiler_params=_params(1),
        name="in_proj",
    )(x, g, w_in)


def _wkv_masks():
    ti = lax.broadcasted_iota(jnp.int32, (CHUNK, PAIR), 0)
    lane = lax.broadcasted_iota(jnp.int32, (CHUNK, PAIR), 1)
    si = lane & (HEAD - 1)
    li = lax.broadcasted_iota(jnp.int32, (CHUNK, CHUNK), 0)
    lj = lax.broadcasted_iota(jnp.int32, (CHUNK, CHUNK), 1)
    head_lo = (lane < HEAD).astype(BF16)
    cum, aa, lvl0, lvls = [], [], [], []
    for reverse in (False, True):
        strict = ((si > ti) if reverse else (si < ti)).astype(BF16)
        incl = ((si >= ti) if reverse else (si <= ti)).astype(BF16)
        cum.append(((lj >= li) if reverse else (lj <= li)).astype(BF16))
        aa.append(jnp.concatenate([jnp.concatenate([strict, strict], axis=1),
                                   jnp.concatenate([incl, incl], axis=1)], axis=0))
        levels = []
        sz = 1
        while sz < CHUNK:
            same_blk = (ti & -(2 * sz)) == (si & -(2 * sz))
            t_hi, s_hi = (ti & sz) != 0, (si & sz) != 0
            levels.append((same_blk & ((~t_hi & s_hi) if reverse else (t_hi & ~s_hi))).astype(BF16))
            sz *= 2
        lvl0.append(levels[0])
        lvls.append(jnp.stack([jnp.concatenate([m * head_lo, m * (1 - head_lo)], axis=0)
                               for m in levels[1:]]))
    bi = lax.broadcasted_iota(jnp.int32, (PAIR, PAIR), 0)
    bj = lax.broadcasted_iota(jnp.int32, (PAIR, PAIR), 1)
    return dict(cum=jnp.stack(cum), aa=jnp.stack(aa), lvl0=jnp.stack(lvl0), lvls=jnp.stack(lvls),
                eye=(si == ti).astype(BF16), heads=jnp.stack([head_lo, 1 - head_lo]),
                same_head=((bi < HEAD) == (bj < HEAD)).astype(F32))


def _wkv_features(z_ref, mu, w2, zc_s, *, reverse, tb):
    z = z_ref[...]
    rows = lax.broadcasted_iota(jnp.int32, z.shape, 0)
    carry = jnp.broadcast_to(zc_s[0:1, :], z.shape)
    if reverse:
        zs = jnp.where(rows == tb - 1, carry, pltpu.roll(z, tb - 1, 0))
        zc_s[...] = jnp.broadcast_to(z[0:1, :], zc_s.shape)
    else:
        zs = jnp.where(rows == 0, carry, pltpu.roll(z, 1, 0))
        zc_s[...] = jnp.broadcast_to(z[tb - 1:tb, :], zc_s.shape)
    zm = z + mu * (zs - z)
    lanes = lax.broadcasted_iota(jnp.int32, z.shape, 1)
    feat = jnp.where(lanes < HEAD, jnp.tanh(zm), zm).astype(BF16)
    return _dot(feat, w2)


def _wkv_prep(lo, k, w0, a0, k_k, k_a, bd):
    d = k.shape[1]
    lw = -DECAY_SCALE * _sigmoid(w0 + lo[:, :d])
    a = _sigmoid(a0 + lo[:, d:])
    kkr = k * k_k
    kkn = kkr * lax.rsqrt(jnp.maximum(_head_sum(kkr * kkr, bd), 1e-24))
    return lw, kkn, kkn * a, k * (1.0 + (a - 1.0) * k_a)


def _wkv_kernel(rf_ref, kf_ref, vf_ref, zf_ref, rb_ref, kb_ref, vb_ref, zb_ref, mu_ref, w0_ref, a0_ref,
                w2_ref, kk_ref, ka_ref, rk_ref, bd_ref, cum_ref, aa_ref, lvl0_ref, lvls_ref, eye_ref,
                heads_ref, same_ref, yf_ref, yb_ref, bonus_ref, state_s, zc_s, *, tb, n_pairs):
    nch = tb // CHUNK
    r_refs, k_refs, v_refs, z_refs, y_refs = ((rf_ref, rb_ref), (kf_ref, kb_ref), (vf_ref, vb_ref),
                                              (zf_ref, zb_ref), (yf_ref, yb_ref))

    @pl.when(pl.program_id(1) == 0)
    def _():
        state_s[...] = jnp.zeros_like(state_s)
        zc_s[...] = jnp.zeros_like(zc_s)

    lo = [_wkv_features(z_refs[dr], mu_ref[dr], w2_ref[dr], zc_s.at[dr], reverse=bool(dr), tb=tb)
          for dr in range(2)]
    edge = (CHUNK - 1, 0)

    lane_lo = lax.broadcasted_iota(jnp.int32, (CHUNK, PAIR), 1) < HEAD

    def stack(x):
        if x.dtype == F32:
            return jnp.concatenate([jnp.where(lane_lo, x, 0.0), jnp.where(lane_lo, 0.0, x)],
                                   axis=0).astype(BF16)
        return jnp.concatenate([x * heads_ref[0], x * heads_ref[1]], axis=0)

    group = math.gcd(nch, CHUNKS_PER_STEP)
    span = group * CHUNK
    n_groups = nch // group
    units = [(dr, p, j) for j in range(group) for dr in range(2) for p in range(n_pairs)]
    halves = (units[:len(units) // 2], units[len(units) // 2:])
    n_levels = lvls_ref.shape[1] + 1
    state = [state_s[i] for i in range(2 * n_pairs)]

    def rows(g, dr, j):
        base = g * span if dr == 0 else tb - (g + 1) * span
        off = base + (j if dr == 0 else group - 1 - j) * CHUNK
        return slice(off, off + CHUNK)

    def head(g, out):
        prep = {}
        for dr in range(2):
            blk = slice(rows(g, dr, 0 if dr == 0 else group - 1).start,
                        rows(g, dr, group - 1 if dr == 0 else 0).stop)
            prep[dr] = _wkv_prep(lo[dr][blk], k_refs[dr][blk, :], w0_ref[dr], a0_ref[dr], kk_ref[...],
                                 ka_ref[...], bd_ref[...])
            if dr == 0:
                rk_sum = _head_sum(rf_ref[blk, :] * kf_ref[blk, :] * rk_ref[...], bd_ref[...])
                bonus_ref[blk, :] = (rk_sum * vf_ref[blk, :]).astype(BF16)
            yield

        def take(which, dr, p, j):
            off = (j if dr == 0 else group - 1 - j) * CHUNK
            return prep[dr][which][off:off + CHUNK, p * PAIR:(p + 1) * PAIR]

        def load(refs, dr, p, j):
            return refs[dr][rows(g, dr, j), p * PAIR:(p + 1) * PAIR]

        for key in ("cw", "rw", "kkw", "v", "a_ab", "a_rb", "a_kv", "tinv", "w_end", "bk_end"):
            out[key] = {}
        for part in halves:
            for un in part:
                dr, p, j = un
                h1, h2 = _split2(take(0, dr, p, j))
                cs = _dot(cum_ref[dr], jnp.concatenate([h1, h2], axis=1))
                out["cw"][un] = cs[:, :PAIR] + cs[:, PAIR:]
            yield
        for part in halves:
            for un in part:
                dr, p, j = un
                cw = out["cw"].pop(un)
                e_pos, e_neg = jnp.exp(cw), jnp.exp(-cw)
                rw = (load(r_refs, dr, p, j) * e_pos).astype(BF16)
                kkw = (take(1, dr, p, j) * jnp.exp(cw - take(0, dr, p, j))).astype(BF16)
                binv = take(2, dr, p, j) * e_neg
                kinv = take(3, dr, p, j) * e_neg
                w_end = e_pos[edge[dr]:edge[dr] + 1, :]
                out["rw"][un], out["kkw"][un], out["w_end"][un] = rw, kkw, w_end
                out["v"][un] = load(v_refs, dr, p, j).astype(BF16)
                out["bk_end"][un] = jnp.concatenate([binv * w_end, kinv * w_end], axis=0).astype(BF16)
                aa = (_dot_nt(jnp.concatenate([kkw, rw], axis=0),
                              jnp.concatenate([stack(binv), stack(kinv)], axis=0)).astype(BF16)
                      * aa_ref[dr])
                a_ab = aa[:CHUNK, :PAIR]
                out["a_ab"][un] = a_ab
                out["a_rb"][un] = aa[CHUNK:, :PAIR]
                out["a_kv"][un] = aa[:, PAIR:]
                out["tinv"][un] = eye_ref[...] - a_ab * lvl0_ref[dr]
            yield

    def tail(g, h):
        nonlocal state
        av, tt = {}, {}
        for part in halves:
            for un in part:
                av[un] = _dot(h["a_kv"][un], stack(h["v"][un]))
            yield
        for part in halves:
            for un in part:
                tt[un] = _dot(h["tinv"][un],
                              jnp.concatenate([stack(h["kkw"][un]), stack(av[un][:CHUNK])], axis=1))
            yield
        for jj in range(group):
            uns = [un for un in units if un[2] == jj]
            ps = [_dot_nt(jnp.concatenate([tt[un][:, :PAIR].astype(BF16), h["rw"][un]], axis=0),
                          s.astype(BF16)) for un, s in zip(uns, state)]
            yield
            u = [-(x[:CHUNK] + tt[un][:, PAIR:]) for un, x in zip(uns, ps)]
            y = [x[CHUNK:] + _dot(h["a_rb"][un], stack(w)) + av[un][CHUNK:]
                 for un, x, w in zip(uns, ps, u)]
            upd = [_dot_tn(jnp.concatenate([x.astype(BF16), h["v"][un]], axis=0), h["bk_end"][un])
                   for un, x in zip(uns, u)]
            yield
            for (dr, p, j), x in zip(uns, y):
                y_refs[dr][rows(g, dr, j), p * PAIR:(p + 1) * PAIR] = x.astype(BF16)
            state = [s * h["w_end"][un] + x * same_ref[...] for un, s, x in zip(uns, state, upd)]
            yield

    def level(h, lvl):
        sz = 1 << lvl
        m1, sel = {}, {}
        for un in units:
            dr = un[0]
            t = h["tinv"][un]
            if sz % BF16_ROWS == 0:
                sel[un] = [r0 for r0 in range(0, CHUNK, sz) if ((r0 & sz) != 0) != bool(dr)]
                t = jnp.concatenate([t[r0:r0 + sz] for r0 in sel[un]], axis=0)
            a2 = jnp.concatenate([h["a_ab"][un], h["a_ab"][un]], axis=0)
            m1[un] = _dot(t, a2 * lvls_ref[dr, lvl - 1])
        yield
        for un in units:
            t = h["tinv"][un]
            m2 = _dot(m1[un].astype(BF16), stack(t)).astype(BF16)
            if un in sel:
                blocks = [t[r0:r0 + sz] for r0 in range(0, CHUNK, sz)]
                for i, r0 in enumerate(sel[un]):
                    blocks[r0 // sz] = blocks[r0 // sz] - m2[i * sz:(i + 1) * sz]
                h["tinv"][un] = jnp.concatenate(blocks, axis=0)
            else:
                h["tinv"][un] = t - m2
        yield

    def advance(gens):
        for gen in gens:
            next(gen, None)

    def drain(gens):
        for gen in gens:
            for _ in gen:
                pass

    heads = [dict() for _ in range(n_groups)]
    drain([head(0, heads[0])])
    side = []
    for g in range(n_groups):
        h = heads[g]
        if g + 1 < n_groups:
            side.append(head(g + 1, heads[g + 1]))
        for lvl in range(1, n_levels):
            for _ in level(h, lvl):
                advance(side)
        drain(side)
        side = [tail(g, h)]
    drain(side)
    for i, s in enumerate(state):
        state_s[i] = s


def _wkv(r, k, v, z, mu, w0, a0, w2, k_k, k_a, r_k, bd, masks, *, bsz, seq, tb):
    n, d = r.shape
    nt = seq // tb
    n_pairs = d // PAIR
    fmap = lambda b, i: (b * nt + i, 0)
    bmap = lambda b, i: (b * nt + nt - 1 - i, 0)
    fblk, bblk = pl.BlockSpec((tb, d), fmap), pl.BlockSpec((tb, d), bmap)
    zf = pl.BlockSpec((tb, PAIR), fmap)
    zb = pl.BlockSpec((tb, PAIR), lambda b, i: (bmap(b, i)[0], 1))
    consts = [masks[key] for key in ("cum", "aa", "lvl0", "lvls", "eye", "heads", "same_head")]
    return pl.pallas_call(
        functools.partial(_wkv_kernel, tb=tb, n_pairs=n_pairs),
        grid=(bsz, nt),
        in_specs=[fblk, fblk, fblk, zf, bblk, bblk, bblk, zb,
                  _full(mu.shape), _full(w0.shape), _full(a0.shape), _full(w2.shape),
                  _full(k_k.shape), _full(k_a.shape), _full(r_k.shape), _full(bd.shape)]
        + [_full(m.shape) for m in consts],
        out_specs=[fblk, bblk, fblk],
        out_shape=[jax.ShapeDtypeStruct((n, d), BF16)] * 3,
        scratch_shapes=[pltpu.VMEM((2 * n_pairs, PAIR, PAIR), F32), pltpu.VMEM((2, SUBLANES, PAIR), F32)],
        compiler_params=_params(2),
        name="wkv",
    )(r, k, v, z, r, k, v, z, mu, w0, a0, w2, k_k, k_a, r_k, bd, *consts)


def _halo_specs(tm, width, rows_per_blk, n_rows):
    nb = tm // rows_per_blk
    last_blk = n_rows // rows_per_blk - 1
    prev = pl.BlockSpec((rows_per_blk, width), lambda i: (jnp.maximum(i * nb - 1, 0), 0))
    nxt = pl.BlockSpec((rows_per_blk, width), lambda i: (jnp.minimum((i + 1) * nb, last_blk), 0))
    return prev, nxt


def _mix_out_kernel(x_ref, yf_ref, yb_ref, bonus_ref, gd_ref, c_ref, cp_ref, cn_ref, bg_ref,
                    gpre_ref, wgate_ref, cw_ref, cb_ref, wa_ref, g2_ref, gnw_ref, gnb_ref,
                    bd_ref, wb_ref, wo_ref, g_ref, o_ref, *, tm, seq):
    i = pl.program_id(0)
    first = (i * tm) % seq == 0
    last = ((i + 1) * tm) % seq == 0
    halo_p, halo_n = cp_ref[...], cn_ref[...]
    c = jnp.concatenate([jnp.where(first, jnp.zeros_like(halo_p), halo_p), c_ref[...],
                         jnp.where(last, jnp.zeros_like(halo_n), halo_n)], axis=0).astype(F32)
    m = tm + 2 * BF16_ROWS
    conv = (pltpu.roll(c, 1, 0) * cw_ref[0:1, :] + c * cw_ref[1:2, :]
            + pltpu.roll(c, m - 1, 0) * cw_ref[2:3, :] + cb_ref[...])[BF16_ROWS:BF16_ROWS + tm]
    y_conv = _dot((bg_ref[...].astype(F32) * conv).astype(BF16), wa_ref[...])
    bd = bd_ref[...]
    y = yf_ref[...].astype(F32) + yb_ref[...].astype(F32)
    mean = _head_sum(y, bd) * (1.0 / HEAD)
    yc = y - mean
    var = _head_sum(yc * yc, bd) * (1.0 / HEAD)
    yn = yc * lax.rsqrt(var + GN_EPS) * gnw_ref[...] + gnb_ref[...]
    gate = _dot(_sigmoid(gd_ref[...]).astype(BF16), g2_ref[...])
    y_rwkv = _dot(((yn + bonus_ref[...].astype(F32)) * gate).astype(BF16), wb_ref[...])
    d_model = x_ref.shape[1]
    u = _rms(x_ref[...], gpre_ref[...]).astype(BF16)
    merged = (_sigmoid(_dot(u, wgate_ref[:, :d_model])) * y_conv
              + _sigmoid(_dot(u, wgate_ref[:, d_model:])) * y_rwkv)
    m = _dot(merged.astype(BF16), wo_ref[...])
    o_ref[...] = x_ref[...] + _rms(m, g_ref[...])


def _mix_out(x, yf, yb, bonus, gd, cghc, bgate, g_pre, w_gates, conv_w, conv_b, w_a, g2, gn_w, gn_b,
             bd, w_b, w_out, g_post, *, tm, seq):
    n, d_model = x.shape
    d = yf.shape[1]
    d_conv = cghc.shape[1]
    row = lambda w: pl.BlockSpec((tm, w), lambda i: (i, 0))
    cp, cn = _halo_specs(tm, d_conv, BF16_ROWS, n)
    return pl.pallas_call(
        functools.partial(_mix_out_kernel, tm=tm, seq=seq),
        grid=(n // tm,),
        in_specs=[row(d_model), row(d), row(d), row(d), row(gd.shape[1]),
                  row(d_conv), cp, cn, row(d_conv), _full(g_pre.shape), _full(w_gates.shape),
                  _full(conv_w.shape), _full(conv_b.shape), _full(w_a.shape), _full(g2.shape),
                  _full(gn_w.shape), _full(gn_b.shape), _full(bd.shape),
                  _full(w_b.shape), _full(w_out.shape), _full(g_post.shape)],
        out_specs=row(d_model),
        out_shape=jax.ShapeDtypeStruct((n, d_model), F32),
        compiler_params=_params(1),
        name="mix_out",
    )(x, yf, yb, bonus, gd, cghc, cghc, cghc, bgate, g_pre, w_gates, conv_w, conv_b, w_a, g2, gn_w,
      gn_b, bd, w_b, w_out, g_post)


def _ffn_kernel(x_ref, xp_ref, xn_ref, p_ref, gpre_ref, wu_ref, cw_ref, cb_ref, wd_ref, gf_ref, wp_ref,
                wg_ref, gp_ref, o_ref, act_s, *, tm, seq, d_ff, col_chunk):
    i = pl.program_id(0)
    first = (i * tm) % seq == 0
    last = ((i + 1) * tm) % seq == 0
    g = gpre_ref[...]
    u_prev = jnp.where(first, 0.0, _rms(xp_ref[...], g))
    u_next = jnp.where(last, 0.0, _rms(xn_ref[...], g))
    u = jnp.concatenate([u_prev, _rms(x_ref[...], g), u_next], axis=0).astype(BF16)
    m = tm + 2 * SUBLANES
    for c0 in range(0, d_ff, col_chunk):
        conv = []
        for off in (c0, d_ff + c0):
            cols = slice(off, off + col_chunk)
            h = _dot(u, wu_ref[:, cols])
            conv.append(pltpu.roll(h, 1, 0) * cw_ref[0:1, cols] + h * cw_ref[1:2, cols]
                        + pltpu.roll(h, m - 1, 0) * cw_ref[2:3, cols] + cb_ref[:, cols])
        hg, hv = conv
        gelu = 0.5 * hg * (1.0 + jnp.tanh(GELU_C * (hg + 0.044715 * (hg * hg * hg))))
        act_s[:, c0:c0 + col_chunk] = (gelu * hv)[SUBLANES:SUBLANES + tm].astype(BF16)
    f = _dot(act_s[...], wd_ref[...])
    x = x_ref[...] + _rms(f, gf_ref[...])
    gate = _sigmoid(_dot(x.astype(BF16), wg_ref[...]))
    pe = _dot(p_ref[...].astype(BF16), wp_ref[...])
    o_ref[...] = x + _rms(gate * pe, gp_ref[...])


def _ffn(x, p, g_pre, w_up, conv_w, conv_b, w_down, g_ffn, w_ple, w_gate, g_ple, *, layer, tm, seq,
         col_chunk):
    n, d_model = x.shape
    d_ff = w_down.shape[0]
    row = lambda w: pl.BlockSpec((tm, w), lambda i: (i, 0))
    xp, xn = _halo_specs(tm, d_model, SUBLANES, n)
    const = lambda a: pl.BlockSpec(a.shape, lambda i: (0,) * a.ndim, pipeline_mode=pl.Buffered(1))
    return pl.pallas_call(
        functools.partial(_ffn_kernel, tm=tm, seq=seq, d_ff=d_ff, col_chunk=col_chunk),
        grid=(n // tm,),
        in_specs=[row(d_model), xp, xn, pl.BlockSpec((None, tm, p.shape[2]), lambda i: (layer, i, 0)),
                  const(g_pre), const(w_up), const(conv_w),
                  const(conv_b), const(w_down), const(g_ffn), const(w_ple), const(w_gate), const(g_ple)],
        out_specs=row(d_model),
        out_shape=jax.ShapeDtypeStruct((n, d_model), F32),
        scratch_shapes=[pltpu.VMEM((tm, d_ff), BF16)],
        compiler_params=_params(1),
        name="ffn",
    )(x, x, x, p, g_pre, w_up, conv_w, conv_b, w_down, g_ffn, w_ple, w_gate, g_ple)


def _tiles(seq):
    tm = min(MXU_WIDTH, seq)
    tm_in = min(2 * MXU_WIDTH, seq)
    tm_mix = min(4 * MXU_WIDTH, seq)
    tb = min(2 * MXU_WIDTH, seq)
    return tm, tm_in, tm_mix, tb


def _layer_weights(i, norm_mix_pre, norm_mix_post, norm_ffn_pre, norm_ffn_post, norm_ple_post, w_in,
                   conv_w, conv_b, w_branch_a, shift_mu, decay_w0, decay_w2, iclr_a0, iclr_a2, gate_g2,
                   k_k, k_a, r_k, gn_w, gn_b, w_branch_b, w_out, w_up, ffn_conv_w, ffn_conv_b, w_down,
                   w_ple, w_ple_gate):
    d_rwkv = k_k.shape[1]
    lora = decay_w2.shape[2]
    row = lambda a: a[i].reshape(1, -1)
    zeros = jnp.zeros((lora, d_rwkv), F32)
    lowrank = [jnp.concatenate([jnp.concatenate([decay_w2[i, d], zeros], axis=1),
                                jnp.concatenate([zeros, iclr_a2[i, d]], axis=1)], axis=0).astype(BF16)
               for d in range(2)]
    head_id = jnp.arange(d_rwkv) // HEAD
    d_model = w_in.shape[1]
    return dict(
        g_mix_pre=row(norm_mix_pre), g_mix_post=row(norm_mix_post), g_ffn_pre=row(norm_ffn_pre),
        g_ffn_post=row(norm_ffn_post), g_ple_post=row(norm_ple_post),
        w_in=w_in[i, :, :-2 * d_model].astype(BF16), w_gates=w_in[i, :, -2 * d_model:].astype(BF16),
        conv_w=conv_w[i], conv_b=row(conv_b),
        w_a=w_branch_a[i].astype(BF16),
        mu=shift_mu[i][:, None, :], w0=decay_w0[i][:, None, :], a0=iclr_a0[i][:, None, :],
        lowrank=jnp.stack(lowrank), g2=gate_g2[i].astype(BF16), k_k=row(k_k), k_a=row(k_a),
        r_k=r_k[i].reshape(1, -1), gn_w=row(gn_w), gn_b=row(gn_b),
        bd=(head_id[:, None] == head_id[None, :]).astype(BF16),
        w_b=w_branch_b[i].astype(BF16), w_out=w_out[i].astype(BF16), w_up=w_up[i].astype(BF16),
        ffn_conv_w=ffn_conv_w[i], ffn_conv_b=row(ffn_conv_b), w_down=w_down[i].astype(BF16),
        w_ple=w_ple[i].astype(BF16), w_gate=w_ple_gate[i].astype(BF16))


def _layer(x, p, lw, masks, *, layer, bsz, seq):
    tm, tm_in, tm_mix, tb = _tiles(seq)
    d_conv = lw["conv_w"].shape[1]
    d_rwkv = lw["k_k"].shape[1]
    d_z = lw["mu"].shape[2]
    d_g = lw["g2"].shape[0]
    d_ff = lw["w_down"].shape[0]
    cghc, bgate, r, k, v, z, gd = _in_proj(
        x, lw["g_mix_pre"], lw["w_in"], tm=tm_in, d_conv=d_conv, d_rwkv=d_rwkv, d_z=2 * d_z, d_g=d_g)
    yf, yb, bonus = _wkv(r, k, v, z, lw["mu"], lw["w0"], lw["a0"], lw["lowrank"], lw["k_k"], lw["k_a"],
                         lw["r_k"], lw["bd"], masks, bsz=bsz, seq=seq, tb=tb)
    x = _mix_out(x, yf, yb, bonus, gd, cghc, bgate, lw["g_mix_pre"], lw["w_gates"], lw["conv_w"], lw["conv_b"],
                 lw["w_a"], lw["g2"], lw["gn_w"], lw["gn_b"], lw["bd"], lw["w_b"],
                 lw["w_out"], lw["g_mix_post"], tm=tm_mix, seq=seq)
    return _ffn(x, p, lw["g_ffn_pre"], lw["w_up"], lw["ffn_conv_w"], lw["ffn_conv_b"], lw["w_down"],
                lw["g_ffn_post"], lw["w_ple"], lw["w_gate"], lw["g_ple_post"], layer=layer, tm=tm,
                seq=seq, col_chunk=math.gcd(d_ff, MXU_WIDTH))


def kernel(x_prompt, x_sample, p_prompt, p_sample, norm_mix_pre, norm_mix_post, norm_ffn_pre, norm_ffn_post, norm_ple_post, w_in, conv_w, conv_b, w_branch_a, shift_mu, decay_w0, decay_w2, iclr_a0, iclr_a2, gate_g2, k_k, k_a, r_k, gn_w, gn_b, w_branch_b, w_out, w_up, ffn_conv_w, ffn_conv_b, w_down, w_ple, w_ple_gate):
    weights = (norm_mix_pre, norm_mix_post, norm_ffn_pre, norm_ffn_post, norm_ple_post, w_in, conv_w,
               conv_b, w_branch_a, shift_mu, decay_w0, decay_w2, iclr_a0, iclr_a2, gate_g2, k_k, k_a,
               r_k, gn_w, gn_b, w_branch_b, w_out, w_up, ffn_conv_w, ffn_conv_b, w_down, w_ple,
               w_ple_gate)
    layers = [_layer_weights(i, *weights) for i in range(w_in.shape[0])]
    masks = _wkv_masks()
    outs = []
    for x, p in ((x_prompt, p_prompt), (x_sample, p_sample)):
        bsz, seq, d_model = x.shape
        y = x.reshape(bsz * seq, d_model)
        p = p.reshape(p.shape[0], bsz * seq, -1)
        for i, lw in enumerate(layers):
            y = _layer(y, p, lw, masks, layer=i, bsz=bsz, seq=seq)
        outs.append(y.reshape(bsz, seq, d_model))
    return tuple(outs)
```
